```python
import math
import jax, jax.numpy as jnp
from jax import lax
import numpy as np

D_MODEL = 1024
BATCH = 4
SEQ = 4096
DEPTH = 1

MIX_WIDTH = 2 * D_MODEL
HEAD_DIM = 64
SSD_WIDTH = MIX_WIDTH // 2
SSD_HEADS = SSD_WIDTH // HEAD_DIM
SSD_GROUPS = 2
SSD_STATE = 128
SSD_CHUNK = 128
CONV_K = 4
CONV_DIM = SSD_WIDTH + 2 * SSD_GROUPS * SSD_STATE
ATTN_WIDTH = MIX_WIDTH - SSD_WIDTH
ATTN_HEADS = ATTN_WIDTH // HEAD_DIM
ATTN_KV_HEADS = ATTN_HEADS // 8
WINDOW = 128
ATTN_BLOCK = WINDOW
D_FF = 4 * D_MODEL
PROJ_WIDTH = SSD_WIDTH + CONV_DIM + SSD_HEADS + ATTN_WIDTH + 2 * ATTN_KV_HEADS * HEAD_DIM
ALPHA = (2.0 * DEPTH) ** 0.25
BETA = (8.0 * DEPTH) ** -0.25
LN_EPS = 1e-5
RMS_EPS = 1e-5

kernel_name = "hymba_ssd_swa_sink_alibi_deepnorm_adaln"


def layer_norm(x, g, b):
    xf = x.astype(jnp.float32)
    mu = jnp.mean(xf, axis=-1, keepdims=True)
    var = jnp.mean(jnp.square(xf - mu), axis=-1, keepdims=True)
    return ((xf - mu) * lax.rsqrt(var + LN_EPS)).astype(x.dtype) * g + b


def gated_group_rmsnorm(y, z, w):
    bsz, L, _ = y.shape
    h = (y * jax.nn.silu(z)).astype(jnp.float32).reshape(bsz, L, SSD_GROUPS, -1)
    h = h * lax.rsqrt(jnp.mean(jnp.square(h), axis=-1, keepdims=True) + RMS_EPS)
    return h.reshape(bsz, L, -1).astype(y.dtype) * w


def causal_dwconv(u, w, b):
    K = w.shape[0]
    y = lax.conv_general_dilated(u, w[:, None, :].astype(u.dtype), window_strides=(1,),
                                 padding=[(K - 1, 0)], dimension_numbers=("NWC", "WIO", "NWC"),
                                 feature_group_count=u.shape[-1])
    return y + b


def segsum(a):
    T = a.shape[-1]
    ae = jnp.broadcast_to(a[..., None], a.shape + (T,))
    ae = jnp.where(jnp.tril(jnp.ones((T, T), dtype=bool), k=-1), ae, 0.0)
    cs = jnp.cumsum(ae, axis=-2)
    return jnp.where(jnp.tril(jnp.ones((T, T), dtype=bool), k=0), cs, -jnp.inf)


def ssd_chunked(xs, dt, A, Bm, Cm):
    b, L, H, P = xs.shape
    G, N = Bm.shape[2], Bm.shape[3]
    R = H // G
    T = SSD_CHUNK
    nc = L // T
    X = (xs * dt[..., None]).reshape(b, nc, T, G, R, P)
    a = jnp.transpose((dt * A).reshape(b, nc, T, G, R), (0, 3, 4, 1, 2))
    a_cum = jnp.cumsum(a, axis=-1)
    Lm = jnp.exp(segsum(a))
    Bc = Bm.reshape(b, nc, T, G, N)
    Cc = Cm.reshape(b, nc, T, G, N)
    cb = jnp.einsum("bclgn,bcsgn->bcgls", Cc, Bc)
    y_diag = jnp.einsum("bcgls,bgrcls,bcsgrp->bclgrp", cb, Lm, X)
    decay_states = jnp.exp(a_cum[..., -1:] - a_cum)
    states = jnp.einsum("bclgn,bgrcl,bclgrp->bcgrpn", Bc, decay_states, X)
    chunk_decay = jnp.exp(a_cum[..., -1])

    def step(S, inp):
        st, dec = inp
        return S * dec[..., None, None] + st, S

    S0 = jnp.zeros((b, G, R, P, N), dtype=X.dtype)
    _, prev = lax.scan(step, S0, (jnp.moveaxis(states, 1, 0), jnp.moveaxis(chunk_decay, -1, 0)))
    prev = jnp.moveaxis(prev, 0, 1)
    y_off = jnp.einsum("bclgn,bcgrpn,bgrcl->bclgrp", Cc, prev, jnp.exp(a_cum))
    return (y_diag + y_off).reshape(b, L, H, P)


def alibi_slopes(n):
    def pow2(m):
        start = 2.0 ** (-8.0 / m)
        return [start ** (i + 1) for i in range(m)]
    if math.log2(n).is_integer():
        s = pow2(n)
    else:
        c = 2 ** math.floor(math.log2(n))
        s = pow2(c) + pow2(2 * c)[0::2][: n - c]
    return jnp.asarray(np.array(s, dtype=np.float32))


def swa_sink_alibi(q, k, v, sinks):
    b, L, H, d = q.shape
    KV = k.shape[2]
    R = H // KV
    W = ATTN_BLOCK
    nb = L // W
    qb = q.reshape(b, nb, W, KV, R, d)
    kb = k.reshape(b, nb, W, KV, d)
    vb = v.reshape(b, nb, W, KV, d)
    prev = lambda t: jnp.concatenate([jnp.zeros_like(t[:, :1]), t[:, :-1]], axis=1)
    kk = jnp.concatenate([prev(kb), kb], axis=2)
    vv = jnp.concatenate([prev(vb), vb], axis=2)
    s = jnp.einsum("bnqkrd,bnskd->bnkrqs", qb, kk).astype(jnp.float32) * (d ** -0.5)
    dist = jnp.arange(W)[:, None] + W - jnp.arange(2 * W)[None, :]
    key_pos = jnp.arange(nb)[:, None] * W + jnp.arange(2 * W)[None, :] - W
    valid = ((dist >= 0) & (dist < WINDOW))[None] & (key_pos >= 0)[:, None, :]
    slopes = alibi_slopes(H).reshape(KV, R)
    s = s - slopes[:, :, None, None] * dist.astype(jnp.float32)
    s = jnp.where(valid[None, :, None, None], s, -jnp.inf)
    sink = sinks.astype(jnp.float32).reshape(KV, R)[:, :, None]
    m = jnp.maximum(jnp.max(s, axis=-1), sink)
    p = jnp.exp(s - m[..., None])
    denom = jnp.sum(p, axis=-1) + jnp.exp(sink - m)
    p = (p / denom[..., None]).astype(v.dtype)
    o = jnp.einsum("bnkrqs,bnskd->bnqkrd", p, vv)
    return o.reshape(b, L, H * d)


def hybrid_mixer(u, w_in, conv_w, conv_b, dt_bias, a_log, d_skip, norm_w, sinks, w_out):
    bsz, L, _ = u.shape
    proj = u @ w_in
    i1 = SSD_WIDTH
    i2 = i1 + CONV_DIM
    i3 = i2 + SSD_HEADS
    i4 = i3 + ATTN_WIDTH
    i5 = i4 + ATTN_KV_HEADS * HEAD_DIM
    z, xbc, dt_raw, q, k, v = jnp.split(proj, [i1, i2, i3, i4, i5], axis=-1)
    xbc = jax.nn.silu(causal_dwconv(xbc, conv_w, conv_b))
    xs, Bm, Cm = jnp.split(xbc, [SSD_WIDTH, SSD_WIDTH + SSD_GROUPS * SSD_STATE], axis=-1)
    f32 = jnp.float32
    dt = jax.nn.softplus(dt_raw.astype(f32) + dt_bias.astype(f32))
    A = -jnp.exp(a_log.astype(f32))
    xs_h = xs.astype(f32).reshape(bsz, L, SSD_HEADS, HEAD_DIM)
    y = ssd_chunked(xs_h, dt, A,
                    Bm.astype(f32).reshape(bsz, L, SSD_GROUPS, SSD_STATE),
                    Cm.astype(f32).reshape(bsz, L, SSD_GROUPS, SSD_STATE))
    y = y + d_skip.astype(f32)[:, None] * xs_h
    y = gated_group_rmsnorm(y.reshape(bsz, L, SSD_WIDTH).astype(u.dtype), z, norm_w)
    o = swa_sink_alibi(q.reshape(bsz, L, ATTN_HEADS, HEAD_DIM),
                       k.reshape(bsz, L, ATTN_KV_HEADS, HEAD_DIM),
                       v.reshape(bsz, L, ATTN_KV_HEADS, HEAD_DIM), sinks)
    return jnp.concatenate([y, o.astype(y.dtype)], axis=-1) @ w_out


def setup_inputs(seed: int = 0) -> dict:
    key = jax.random.key(seed)
    ks = jax.random.split(key, 24)
    nrm = lambda k, shape, s: jax.random.normal(k, shape, jnp.float32) * s
    Dp = DEPTH
    dt0 = jnp.exp(jax.random.uniform(ks[10], (Dp, SSD_HEADS), jnp.float32,
                                     math.log(1e-3), math.log(1e-1)))
    dt_bias = dt0 + jnp.log(-jnp.expm1(-dt0))
    a_log = jnp.log(jax.random.uniform(ks[11], (Dp, SSD_HEADS), jnp.float32, 1.0, 16.0))
    return {
        "x": nrm(ks[0], (BATCH, SEQ, D_MODEL), 1.0),
        "c": nrm(ks[1], (BATCH, D_MODEL), 1.0),
        "ln_in_g": 1.0 + nrm(ks[2], (D_MODEL,), 0.02),
        "ln_in_b": nrm(ks[3], (D_MODEL,), 0.02),
        "ada_w": nrm(ks[4], (Dp, D_MODEL, 6 * D_MODEL), 0.1 * D_MODEL ** -0.5),
        "ada_b": nrm(ks[5], (Dp, 6 * D_MODEL), 0.02),
        "w_in": nrm(ks[6], (Dp, D_MODEL, PROJ_WIDTH), D_MODEL ** -0.5),
        "conv_w": nrm(ks[7], (Dp, CONV_K, CONV_DIM), CONV_K ** -0.5),
        "conv_b": nrm(ks[8], (Dp, CONV_DIM), 0.02),
        "dt_bias": dt_bias,
        "a_log": a_log,
        "d_skip": 1.0 + nrm(ks[12], (Dp, SSD_HEADS), 0.1),
        "ssd_norm_w": 1.0 + nrm(ks[13], (Dp, SSD_WIDTH), 0.02),
        "attn_sinks": nrm(ks[14], (Dp, ATTN_HEADS), 0.5),
        "w_out": nrm(ks[15], (Dp, MIX_WIDTH, D_MODEL), BETA * MIX_WIDTH ** -0.5),
        "ln1_g": 1.0 + nrm(ks[16], (Dp, D_MODEL), 0.02),
        "ln1_b": nrm(ks[17], (Dp, D_MODEL), 0.02),
        "w_ff1": nrm(ks[18], (Dp, D_MODEL, D_FF), D_MODEL ** -0.5),
        "b_ff1": nrm(ks[19], (Dp, D_FF), 0.02),
        "w_ff2": nrm(ks[20], (Dp, D_FF, D_MODEL), BETA * D_FF ** -0.5),
        "b_ff2": nrm(ks[21], (Dp, D_MODEL), 0.02),
        "ln2_g": 1.0 + nrm(ks[22], (Dp, D_MODEL), 0.02),
        "ln2_b": nrm(ks[23], (Dp, D_MODEL), 0.02),
    }


def reference(x, c, ln_in_g, ln_in_b, ada_w, ada_b, w_in, conv_w, conv_b, dt_bias, a_log, d_skip,
              ssd_norm_w, attn_sinks, w_out, ln1_g, ln1_b, w_ff1, b_ff1, w_ff2, b_ff2, ln2_g, ln2_b):
    h = layer_norm(x, ln_in_g, ln_in_b)
    cs = jax.nn.silu(c)
    for l in range(DEPTH):
        mod = cs @ ada_w[l] + ada_b[l]
        sh1, sc1, g1, sh2, sc2, g2 = jnp.split(mod[:, None, :], 6, axis=-1)
        u = h * (1.0 + sc1) + sh1
        mix = hybrid_mixer(u, w_in[l], conv_w[l], conv_b[l], dt_bias[l], a_log[l], d_skip[l],
                           ssd_norm_w[l], attn_sinks[l], w_out[l])
        h = layer_norm(ALPHA * h + (1.0 + g1) * mix, ln1_g[l], ln1_b[l])
        u = h * (1.0 + sc2) + sh2
        f = jnp.square(jax.nn.relu(u @ w_ff1[l] + b_ff1[l])) @ w_ff2[l] + b_ff2[l]
        h = layer_norm(ALPHA * h + (1.0 + g2) * f, ln2_g[l], ln2_b[l])
    return h
```

```python
import functools
import math

import numpy as np
import jax
import jax.numpy as jnp
from jax import lax
from jax.experimental import pallas as pl
from jax.experimental.pallas import tpu as pltpu

D_MODEL = 1024
HEAD_DIM = 64
SSD_WIDTH = 1024
SSD_HEADS = 16
SSD_GROUPS = 2
SSD_STATE = 128
CHUNK = 128
CONV_K = 4
CONV_DIM = SSD_WIDTH + 2 * SSD_GROUPS * SSD_STATE
ATTN_WIDTH = 1024
ATTN_HEADS = 16
ATTN_KV_HEADS = 2
KV_WIDTH = ATTN_KV_HEADS * HEAD_DIM
D_FF = 4096
MIX_WIDTH = SSD_WIDTH + ATTN_WIDTH
LANES = 128
DT_PAD = LANES
PROJ_PAD = SSD_WIDTH + CONV_DIM + ATTN_WIDTH + 2 * KV_WIDTH + DT_PAD
ALPHA = 2.0 ** 0.25
LN_EPS = 1e-5
RMS_EPS = 1e-5
VMEM_LIMIT = 56 * 1024 * 1024

F32 = jnp.float32
BF16 = jnp.bfloat16


def _alibi_slopes(n):
    start = 2.0 ** (-8.0 / n)
    return [float(np.float32(start ** (i + 1))) for i in range(n)]


SLOPES = _alibi_slopes(ATTN_HEADS)


def _layer_norm(x, g, b):
    mu = jnp.mean(x, axis=-1, keepdims=True)
    xc = x - mu
    var = jnp.mean(xc * xc, axis=-1, keepdims=True)
    return xc * lax.rsqrt(var + LN_EPS) * g + b


def _silu(x):
    return x * (1.0 / (1.0 + jnp.exp(-x)))


def _adaln_kernel(c_ref, w_ref, b_ref, o_ref):
    cs = _silu(c_ref[...]).astype(BF16)
    o_ref[...] = jnp.dot(cs, w_ref[...].astype(BF16), preferred_element_type=F32) + b_ref[...]


def _adaln_mod(c_pad, ada_w, ada_b):
    rows, n = c_pad.shape[0], ada_w.shape[1]
    tn = 1024
    return pl.pallas_call(
        _adaln_kernel,
        grid=(n // tn,),
        in_specs=[pl.BlockSpec((rows, D_MODEL), lambda j: (0, 0)),
                  pl.BlockSpec((D_MODEL, tn), lambda j: (0, j)),
                  pl.BlockSpec((1, tn), lambda j: (0, j))],
        out_specs=pl.BlockSpec((rows, tn), lambda j: (0, j)),
        out_shape=jax.ShapeDtypeStruct((rows, n), F32),
        compiler_params=pltpu.CompilerParams(dimension_semantics=("arbitrary",),
                                             vmem_limit_bytes=VMEM_LIMIT),
        name="adaln_mod",
    )(c_pad, ada_w, ada_b)


_OFF_Z = 0
_OFF_XBC = _OFF_Z + SSD_WIDTH
_OFF_Q = _OFF_XBC + CONV_DIM
_OFF_K = _OFF_Q + ATTN_WIDTH
_OFF_V = _OFF_K + KV_WIDTH
_OFF_DT = _OFF_V + KV_WIDTH


def _inproj_kernel(x_ref, mod_ref, g_ref, b_ref, w_ref, z_ref, xbc_ref, q_ref, k_ref, v_ref, dt_ref):
    h = _layer_norm(x_ref[...], g_ref[...], b_ref[...])
    sh1 = mod_ref[:, 0:D_MODEL]
    sc1 = mod_ref[:, D_MODEL:2 * D_MODEL]
    u = (h * (1.0 + sc1) + sh1).astype(BF16)

    def proj(off, width):
        return jnp.dot(u, w_ref[:, off:off + width], preferred_element_type=F32)

    z_ref[...] = proj(_OFF_Z, SSD_WIDTH)
    xbc_ref[...] = proj(_OFF_XBC, CONV_DIM)
    q_ref[...] = proj(_OFF_Q, ATTN_WIDTH).astype(BF16)
    k_ref[...] = proj(_OFF_K, KV_WIDTH).astype(BF16)
    v_ref[...] = proj(_OFF_V, KV_WIDTH).astype(BF16)
    dt_ref[...] = proj(_OFF_DT, DT_PAD)


def _in_proj(x2, mod3, ln_g, ln_b, w_all, seq, tm):
    n_tok = x2.shape[0]
    tiles_per_seq = seq // tm
    row = lambda i: (i, 0)
    const = lambda i: (0, 0)
    widths = [(SSD_WIDTH, F32), (CONV_DIM, F32), (ATTN_WIDTH, BF16), (KV_WIDTH, BF16), (KV_WIDTH, BF16),
              (DT_PAD, F32)]
    return pl.pallas_call(
        _inproj_kernel,
        grid=(n_tok // tm,),
        in_specs=[pl.BlockSpec((tm, D_MODEL), row),
                  pl.BlockSpec((None, 1, 6 * D_MODEL), lambda i: (i // tiles_per_seq, 0, 0)),
                  pl.BlockSpec((1, D_MODEL), const),
                  pl.BlockSpec((1, D_MODEL), const),
                  pl.BlockSpec((D_MODEL, PROJ_PAD), const)],
        out_specs=[pl.BlockSpec((tm, w), row) for w, _ in widths],
        out_shape=[jax.ShapeDtypeStruct((n_tok, w), dt) for w, dt in widths],
        compiler_params=pltpu.CompilerParams(dimension_semantics=("arbitrary",),
                                             vmem_limit_bytes=VMEM_LIMIT),
        name="in_proj",
    )(x2, mod3, ln_g, ln_b, w_all)


def _mixer_kernel(sinks_ref, z_ref, xbc_ref, dt_ref, q_ref, k_ref, v_ref,
                  convw_ref, convb_ref, dtb_ref, alog_ref, dskip_ref, normw_ref,
                  y_ref,
                  xpad_ref, state_ref, kprev_ref, vprev_ref, ybuf_ref):
    c = pl.program_id(1)
    T = CHUNK

    @pl.when(c == 0)
    def _():
        xpad_ref[0:8, :] = jnp.zeros((8, CONV_DIM), F32)
        state_ref[...] = jnp.zeros_like(state_ref)
        kprev_ref[...] = jnp.zeros_like(kprev_ref)
        vprev_ref[...] = jnp.zeros_like(vprev_ref)

    xpad_ref[8:8 + T, :] = xbc_ref[...]
    conv = convb_ref[...] + convw_ref[3:4, :] * xpad_ref[8:8 + T, :]
    for kk in range(CONV_K - 1):
        conv = conv + convw_ref[kk:kk + 1, :] * xpad_ref[5 + kk:5 + kk + T, :]
    xpad_ref[0:8, :] = xpad_ref[T:T + 8, :]
    xc = _silu(conv)

    dt_in = dt_ref[...] + dtb_ref[...]
    dt = jnp.maximum(dt_in, 0.0) + jnp.log(1.0 + jnp.exp(-jnp.abs(dt_in)))
    a = dt * (-jnp.exp(alog_ref[...]))
    row = lax.broadcasted_iota(jnp.int32, (T, T), 0)
    col = lax.broadcasted_iota(jnp.int32, (T, T), 1)
    causal = row >= col
    tril = jnp.where(causal, 1.0, 0.0).astype(F32)
    acum = jnp.dot(tril, a, preferred_element_type=F32, precision=lax.Precision.HIGHEST)
    total = acum[T - 1:T, :]
    acum_t = acum.T
    dt_t = dt.T
    wst_t = (jnp.exp(total - acum) * dt).T
    e_acum = jnp.exp(acum)
    e_total = jnp.exp(total)
    lane = lax.broadcasted_iota(jnp.int32, (T, LANES), 1)
    lane_lo = lane < HEAD_DIM
    lane2 = lax.broadcasted_iota(jnp.int32, (2 * T, LANES), 1)
    lane2_lo = lane2 < HEAD_DIM

    for g in range(SSD_GROUPS):
        b_g = xc[:, SSD_WIDTH + g * SSD_STATE:SSD_WIDTH + (g + 1) * SSD_STATE]
        c_g = xc[:, SSD_WIDTH + SSD_GROUPS * SSD_STATE + g * SSD_STATE:
                 SSD_WIDTH + SSD_GROUPS * SSD_STATE + (g + 1) * SSD_STATE]
        cb = lax.dot_general(c_g.astype(BF16), b_g.astype(BF16), (((1,), (1,)), ((), ())),
                             preferred_element_type=F32)
        b_t = b_g.T
        s_prev = state_ref[g]
        y_off = jnp.dot(c_g.astype(BF16), s_prev.astype(BF16), preferred_element_type=F32)
        for j in range(4):
            h0 = g * 8 + 2 * j
            slab = g * 4 + j
            xs_pair = xc[:, slab * LANES:(slab + 1) * LANES]
            rhs = jnp.concatenate([jnp.where(lane_lo, xs_pair, 0.0),
                                   jnp.where(lane_lo, 0.0, xs_pair)], axis=0).astype(BF16)
            m_parts, s_parts = [], []
            for hh in (h0, h0 + 1):
                diff = acum[:, hh:hh + 1] - acum_t[hh:hh + 1, :]
                decay = jnp.exp(jnp.where(causal, diff, -jnp.inf))
                m_parts.append(cb * decay * dt_t[hh:hh + 1, :])
                s_parts.append(b_t * wst_t[hh:hh + 1, :])
            lhs_y = jnp.concatenate(m_parts, axis=1).astype(BF16)
            lhs_s = jnp.concatenate(s_parts, axis=1).astype(BF16)
            y_diag = jnp.dot(lhs_y, rhs, preferred_element_type=F32)
            st_new = jnp.dot(lhs_s, rhs, preferred_element_type=F32)
            off_scale = jnp.where(lane_lo, e_acum[:, h0:h0 + 1], e_acum[:, h0 + 1:h0 + 2])
            y_pair = (y_diag + off_scale * y_off[:, j * LANES:(j + 1) * LANES]
                      + dskip_ref[:, slab * LANES:(slab + 1) * LANES] * xs_pair)
            ybuf_ref[:, slab * LANES:(slab + 1) * LANES] = y_pair
            cd = jnp.where(lane_lo[0:1, :], e_total[:, h0:h0 + 1], e_total[:, h0 + 1:h0 + 2])
            state_ref[g, :, j * LANES:(j + 1) * LANES] = s_prev[:, j * LANES:(j + 1) * LANES] * cd + st_new

    gw = SSD_WIDTH // SSD_GROUPS
    for g in range(SSD_GROUPS):
        hgate = ybuf_ref[:, g * gw:(g + 1) * gw] * _silu(z_ref[:, g * gw:(g + 1) * gw])
        ms = jnp.mean(hgate * hgate, axis=-1, keepdims=True)
        y_ref[:, g * gw:(g + 1) * gw] = (hgate * lax.rsqrt(ms + RMS_EPS)
                                         * normw_ref[:, g * gw:(g + 1) * gw]).astype(BF16)

    k_cur = k_ref[...]
    v_cur = v_ref[...]
    kk = jnp.concatenate([kprev_ref[...], k_cur], axis=0).astype(F32)
    vv = jnp.concatenate([vprev_ref[...], v_cur], axis=0).astype(F32)
    kk_sw = pltpu.roll(kk, HEAD_DIM, axis=1)
    vv_sw = pltpu.roll(vv, HEAD_DIM, axis=1)
    kprev_ref[...] = k_cur
    vprev_ref[...] = v_cur

    qi = lax.broadcasted_iota(jnp.int32, (T, 2 * T), 0)
    kj = lax.broadcasted_iota(jnp.int32, (T, 2 * T), 1)
    dist = qi + T - kj
    first_key = jnp.where(c == 0, T, 0)
    valid = (dist >= 0) & (dist < T) & (kj >= first_key)
    dist_f = dist.astype(F32)
    scale = HEAD_DIM ** -0.5

    for g in range(ATTN_KV_HEADS):
        own, other = (kk, kk_sw) if g == 0 else (kk_sw, kk)
        kk2 = jnp.concatenate([jnp.where(lane2_lo, own, 0.0), jnp.where(lane2_lo, 0.0, other)],
                              axis=0).astype(BF16)
        own, other = (vv, vv_sw) if g == 0 else (vv_sw, vv)
        vv2 = jnp.concatenate([jnp.where(lane2_lo, own, 0.0), jnp.where(lane2_lo, 0.0, other)],
                              axis=0).astype(BF16)
        qs = jnp.concatenate([q_ref[:, (g * 4 + j) * LANES:(g * 4 + j + 1) * LANES] for j in range(4)], axis=0)
        s_all = lax.dot_general(qs, kk2, (((1,), (1,)), ((), ())), preferred_element_type=F32)
        p_rows, inv_rows = [], []
        for j in range(4):
            p_halves, inv_halves = [], []
            for half in range(2):
                hh = g * 8 + 2 * j + half
                s = s_all[j * T:(j + 1) * T, half * 2 * T:(half + 1) * 2 * T] * scale - SLOPES[hh] * dist_f
                s = jnp.where(valid, s, -jnp.inf)
                sink = sinks_ref[hh]
                m = jnp.maximum(jnp.max(s, axis=-1, keepdims=True), sink)
                p = jnp.exp(s - m)
                den = jnp.sum(p, axis=-1, keepdims=True) + jnp.exp(sink - m)
                p_halves.append(p.astype(BF16))
                inv_halves.append(1.0 / den)
            p_rows.append(jnp.concatenate(p_halves, axis=1))
            inv_rows.append(jnp.where(lane_lo, inv_halves[0], inv_halves[1]))
        p_all = jnp.concatenate(p_rows, axis=0)
        o_all = jnp.dot(p_all, vv2, preferred_element_type=F32)
        for j in range(4):
            col0 = SSD_WIDTH + (g * 4 + j) * LANES
            y_ref[:, col0:col0 + LANES] = (o_all[j * T:(j + 1) * T, :] * inv_rows[j]).astype(BF16)


def _mixer(sinks, z, xbc, dt, q, k, v, conv_w, conv_b, dtb, alog, dskip, normw, batch, seq):
    n_chunks = seq // CHUNK
    n_tok = batch * seq
    row = lambda b, c: (b * n_chunks + c, 0)
    const = lambda b, c: (0, 0)
    vspec = lambda w: pl.BlockSpec((CHUNK, w), row)
    pspec = lambda r, w: pl.BlockSpec((r, w), const)
    return pl.pallas_call(
        _mixer_kernel,
        grid=(batch, n_chunks),
        in_specs=[pl.BlockSpec(memory_space=pltpu.SMEM),
                  vspec(SSD_WIDTH), vspec(CONV_DIM), vspec(DT_PAD), vspec(ATTN_WIDTH), vspec(KV_WIDTH),
                  vspec(KV_WIDTH),
                  pspec(CONV_K, CONV_DIM), pspec(1, CONV_DIM), pspec(1, DT_PAD), pspec(1, DT_PAD),
                  pspec(1, SSD_WIDTH), pspec(1, SSD_WIDTH)],
        out_specs=pl.BlockSpec((CHUNK, MIX_WIDTH), row),
        out_shape=jax.ShapeDtypeStruct((n_tok, MIX_WIDTH), BF16),
        scratch_shapes=[pltpu.VMEM((CHUNK + 8, CONV_DIM), F32),
                        pltpu.VMEM((SSD_GROUPS, SSD_STATE, SSD_WIDTH // SSD_GROUPS), F32),
                        pltpu.VMEM((CHUNK, KV_WIDTH), BF16),
                        pltpu.VMEM((CHUNK, KV_WIDTH), BF16),
                        pltpu.VMEM((CHUNK, SSD_WIDTH), F32)],
        compiler_params=pltpu.CompilerParams(dimension_semantics=("arbitrary", "arbitrary"),
                                             vmem_limit_bytes=VMEM_LIMIT),
        name="mixer",
    )(sinks, z, xbc, dt, q, k, v, conv_w, conv_b, dtb, alog, dskip, normw)


FF_CHUNK = 1024


def _out_ffn_kernel(x_ref, y_ref, mod_ref, lng_ref, lnb_ref, wout_ref, ln1g_ref, ln1b_ref,
                    w1_ref, b1_ref, w2_ref, b2_ref, ln2g_ref, ln2b_ref, o_ref):
    h0 = _layer_norm(x_ref[...], lng_ref[...], lnb_ref[...])
    mix = jnp.dot(y_ref[...], wout_ref[...], preferred_element_type=F32)
    g1 = mod_ref[:, 2 * D_MODEL:3 * D_MODEL]
    sh2 = mod_ref[:, 3 * D_MODEL:4 * D_MODEL]
    sc2 = mod_ref[:, 4 * D_MODEL:5 * D_MODEL]
    g2 = mod_ref[:, 5 * D_MODEL:6 * D_MODEL]
    h1 = _layer_norm(ALPHA * h0 + (1.0 + g1) * mix, ln1g_ref[...], ln1b_ref[...])
    u = (h1 * (1.0 + sc2) + sh2).astype(BF16)
    f = jnp.zeros_like(h1)
    for ck in range(D_FF // FF_CHUNK):
        sl = slice(ck * FF_CHUNK, (ck + 1) * FF_CHUNK)
        hid = jnp.dot(u, w1_ref[:, sl], preferred_element_type=F32) + b1_ref[:, sl]
        hid = jnp.square(jnp.maximum(hid, 0.0)).astype(BF16)
        f = f + jnp.dot(hid, w2_ref[sl, :], preferred_element_type=F32)
    f = f + b2_ref[...]
    o_ref[...] = _layer_norm(ALPHA * h1 + (1.0 + g2) * f, ln2g_ref[...], ln2b_ref[...])


def _out_ffn(x2, ycat, mod3, ln_g, ln_b, w_out, ln1_g, ln1_b, w1, b1, w2, b2, ln2_g, ln2_b, seq, tm):
    n_tok = x2.shape[0]
    tiles_per_seq = seq // tm
    row = lambda i: (i, 0)
    const = lambda i: (0, 0)
    resident = lambda r, w: pl.BlockSpec((r, w), const, pipeline_mode=pl.Buffered(1))
    return pl.pallas_call(
        _out_ffn_kernel,
        grid=(n_tok // tm,),
        in_specs=[pl.BlockSpec((tm, D_MODEL), row),
                  pl.BlockSpec((tm, MIX_WIDTH), row),
                  pl.BlockSpec((None, 1, 6 * D_MODEL), lambda i: (i // tiles_per_seq, 0, 0)),
                  resident(1, D_MODEL), resident(1, D_MODEL),
                  resident(MIX_WIDTH, D_MODEL),
                  resident(1, D_MODEL), resident(1, D_MODEL),
                  resident(D_MODEL, D_FF), resident(1, D_FF),
                  resident(D_FF, D_MODEL), resident(1, D_MODEL),
                  resident(1, D_MODEL), resident(1, D_MODEL)],
        out_specs=pl.BlockSpec((tm, D_MODEL), row),
        out_shape=jax.ShapeDtypeStruct((n_tok, D_MODEL), F32),
        compiler_params=pltpu.CompilerParams(dimension_semantics=("arbitrary",),
                                             vmem_limit_bytes=VMEM_LIMIT),
        name="out_ffn",
    )(x2, ycat, mod3, ln_g, ln_b, w_out, ln1_g, ln1_b, w1, b1, w2, b2, ln2_g, ln2_b)


def kernel(x, c, ln_in_g, ln_in_b, ada_w, ada_b, w_in, conv_w, conv_b, dt_bias, a_log, d_skip, ssd_norm_w,
           attn_sinks, w_out, ln1_g, ln1_b, w_ff1, b_ff1, w_ff2, b_ff2, ln2_g, ln2_b):
    batch, seq, _ = x.shape
    depth = ada_w.shape[0]
    assert depth == 1 and seq % CHUNK == 0
    n_tok = batch * seq
    row2 = lambda p: p.reshape(1, -1)
    x2 = x.reshape(n_tok, D_MODEL)
    ln_g, ln_b = row2(ln_in_g), row2(ln_in_b)

    c_pad = jnp.pad(c, ((0, 8 - batch), (0, 0)))
    mod = _adaln_mod(c_pad, ada_w[0], row2(ada_b[0]))
    mod3 = mod[:batch].reshape(batch, 1, 6 * D_MODEL)

    wi = w_in[0]
    i1 = SSD_WIDTH
    i2 = i1 + CONV_DIM
    i3 = i2 + SSD_HEADS
    w_all = jnp.concatenate([wi[:, :i2], wi[:, i3:], wi[:, i2:i3],
                             jnp.zeros((D_MODEL, DT_PAD - SSD_HEADS), wi.dtype)], axis=1).astype(BF16)

    z, xbc, q, k, v, dt = _in_proj(x2, mod3, ln_g, ln_b, w_all, seq, tm=512)

    pad_heads = lambda p: jnp.pad(p, (0, DT_PAD - SSD_HEADS)).reshape(1, DT_PAD)
    ycat = _mixer(attn_sinks[0], z, xbc, dt, q, k, v, conv_w[0], row2(conv_b[0]),
                  pad_heads(dt_bias[0]), pad_heads(a_log[0]),
                  jnp.repeat(d_skip[0], HEAD_DIM).reshape(1, SSD_WIDTH), row2(ssd_norm_w[0]), batch, seq)

    out = _out_ffn(x2, ycat, mod3, ln_g, ln_b, w_out[0].astype(BF16), row2(ln1_g[0]), row2(ln1_b[0]),
                   w_ff1[0].astype(BF16), row2(b_ff1[0]), w_ff2[0].astype(BF16), row2(b_ff2[0]),
                   row2(ln2_g[0]), row2(ln2_b[0]), seq, tm=512)
    return out.reshape(batch, seq, D_MODEL)
```

```python
import functools
import math

import numpy as np
import jax
import jax.numpy as jnp
from jax import lax
from jax.experimental import pallas as pl
from jax.experimental.pallas import tpu as pltpu

D_MODEL = 1024
HEAD_DIM = 64
SSD_WIDTH = 1024
SSD_HEADS = 16
SSD_GROUPS = 2
SSD_STATE = 128
CHUNK = 128
CONV_K = 4
CONV_DIM = SSD_WIDTH + 2 * SSD_GROUPS * SSD_STATE
ATTN_WIDTH = 1024
ATTN_HEADS = 16
ATTN_KV_HEADS = 2
KV_WIDTH = ATTN_KV_HEADS * HEAD_DIM
D_FF = 4096
MIX_WIDTH = SSD_WIDTH + ATTN_WIDTH
LANES = 128
DT_PAD = LANES
PROJ_PAD = SSD_WIDTH + CONV_DIM + ATTN_WIDTH + 2 * KV_WIDTH + DT_PAD
ALPHA = 2.0 ** 0.25
LOG2E = math.log2(math.e)
Q_SCALE = HEAD_DIM ** -0.5 * LOG2E
LN_EPS = 1e-5
RMS_EPS = 1e-5
VMEM_LIMIT = 56 * 1024 * 1024

F32 = jnp.float32
BF16 = jnp.bfloat16


def _alibi_slopes(n):
    start = 2.0 ** (-8.0 / n)
    return [float(np.float32(start ** (i + 1))) for i in range(n)]


SLOPES = _alibi_slopes(ATTN_HEADS)


def _layer_norm(x, g, b):
    mu = jnp.mean(x, axis=-1, keepdims=True)
    xc = x - mu
    var = jnp.mean(xc * xc, axis=-1, keepdims=True)
    return xc * lax.rsqrt(var + LN_EPS) * g + b


def _silu(x):
    return x * (1.0 / (1.0 + jnp.exp(-x)))


def _adaln_kernel(c_ref, w_ref, b_ref, o_ref):
    cs = _silu(c_ref[...]).astype(BF16)
    o_ref[...] = jnp.dot(cs, w_ref[...].astype(BF16), preferred_element_type=F32) + b_ref[...]


def _adaln_mod(c_pad, ada_w, ada_b):
    rows, n = c_pad.shape[0], ada_w.shape[1]
    tn = 1024
    return pl.pallas_call(
        _adaln_kernel,
        grid=(n // tn,),
        in_specs=[pl.BlockSpec((rows, D_MODEL), lambda j: (0, 0)),
                  pl.BlockSpec((D_MODEL, tn), lambda j: (0, j)),
                  pl.BlockSpec((1, tn), lambda j: (0, j))],
        out_specs=pl.BlockSpec((rows, tn), lambda j: (0, j)),
        out_shape=jax.ShapeDtypeStruct((rows, n), F32),
        compiler_params=pltpu.CompilerParams(dimension_semantics=("arbitrary",),
                                             vmem_limit_bytes=VMEM_LIMIT),
        name="adaln_mod",
    )(c_pad, ada_w, ada_b)


_OFF_Z = 0
_OFF_XBC = _OFF_Z + SSD_WIDTH
_OFF_Q = _OFF_XBC + CONV_DIM
_OFF_K = _OFF_Q + ATTN_WIDTH
_OFF_V = _OFF_K + KV_WIDTH
_OFF_DT = _OFF_V + KV_WIDTH


def _inproj_kernel(x_ref, mod_ref, g_ref, b_ref, w_ref, convw_ref, convb_ref,
                   z_ref, xbc_ref, q_ref, k_ref, v_ref, dt_ref, xpad_ref, *, tiles_per_seq):
    tm = x_ref.shape[0]
    h = _layer_norm(x_ref[...], g_ref[...], b_ref[...])
    sh1 = mod_ref[:, 0:D_MODEL]
    sc1 = mod_ref[:, D_MODEL:2 * D_MODEL]
    u = (h * (1.0 + sc1) + sh1).astype(BF16)

    def proj(off, width):
        return jnp.dot(u, w_ref[:, off:off + width], preferred_element_type=F32)

    @pl.when(pl.program_id(0) % tiles_per_seq == 0)
    def _():
        xpad_ref[0:8, :] = jnp.zeros((8, CONV_DIM), F32)

    xpad_ref[8:8 + tm, :] = proj(_OFF_XBC, CONV_DIM)
    conv = convb_ref[...] + convw_ref[3:4, :] * xpad_ref[8:8 + tm, :]
    for kk in range(CONV_K - 1):
        conv = conv + convw_ref[kk:kk + 1, :] * xpad_ref[5 + kk:5 + kk + tm, :]
    xpad_ref[0:8, :] = xpad_ref[tm:tm + 8, :]
    xbc_ref[...] = _silu(conv)

    z_ref[...] = _silu(proj(_OFF_Z, SSD_WIDTH))
    q_ref[...] = (proj(_OFF_Q, ATTN_WIDTH) * Q_SCALE).astype(BF16)
    k_ref[...] = proj(_OFF_K, KV_WIDTH).astype(BF16)
    v_ref[...] = proj(_OFF_V, KV_WIDTH).astype(BF16)
    dt_ref[...] = proj(_OFF_DT, DT_PAD)


def _in_proj(x2, mod3, ln_g, ln_b, w_all, conv_w, conv_b, seq, tm):
    n_tok = x2.shape[0]
    tiles_per_seq = seq // tm
    row = lambda i: (i, 0)
    const = lambda i: (0, 0)
    widths = [(SSD_WIDTH, F32), (CONV_DIM, F32), (ATTN_WIDTH, BF16), (KV_WIDTH, BF16), (KV_WIDTH, BF16),
              (DT_PAD, F32)]
    return pl.pallas_call(
        functools.partial(_inproj_kernel, tiles_per_seq=tiles_per_seq),
        grid=(n_tok // tm,),
        in_specs=[pl.BlockSpec((tm, D_MODEL), row),
                  pl.BlockSpec((None, 1, 6 * D_MODEL), lambda i: (i // tiles_per_seq, 0, 0)),
                  pl.BlockSpec((1, D_MODEL), const),
                  pl.BlockSpec((1, D_MODEL), const),
                  pl.BlockSpec((D_MODEL, PROJ_PAD), const, pipeline_mode=pl.Buffered(1)),
                  pl.BlockSpec((CONV_K, CONV_DIM), const),
                  pl.BlockSpec((1, CONV_DIM), const)],
        out_specs=[pl.BlockSpec((tm, w), row) for w, _ in widths],
        out_shape=[jax.ShapeDtypeStruct((n_tok, w), dt) for w, dt in widths],
        scratch_shapes=[pltpu.VMEM((tm + 8, CONV_DIM), F32)],
        compiler_params=pltpu.CompilerParams(dimension_semantics=("arbitrary",),
                                             vmem_limit_bytes=VMEM_LIMIT),
        name="in_proj",
    )(x2, mod3, ln_g, ln_b, w_all, conv_w, conv_b)


def _ssd_chunk(rows, xc_ref, z_ref, dt_ref, dtb_ref, alog_ref, dskip_ref, normw_ref, y_ref, state_ref, ybuf_ref):
    T = CHUNK
    dt_in = dt_ref[rows, :] + dtb_ref[...]
    dt = jnp.maximum(dt_in, 0.0) + jnp.log(1.0 + jnp.exp(-jnp.abs(dt_in)))
    a = dt * (-jnp.exp(alog_ref[...]))
    row = lax.broadcasted_iota(jnp.int32, (T, T), 0)
    col = lax.broadcasted_iota(jnp.int32, (T, T), 1)
    causal = row >= col
    tril = jnp.where(causal, 1.0, 0.0).astype(F32)
    acum = jnp.dot(tril, a, preferred_element_type=F32, precision=lax.Precision.HIGHEST)
    total = acum[T - 1:T, :]
    acol2 = acum * LOG2E
    arow2 = ((acum - jnp.log(dt)) * LOG2E).T
    wst_t = (jnp.exp(total - acum) * dt).T
    e_acum = jnp.exp(acum)
    e_total = jnp.exp(total)
    lane_lo = lax.broadcasted_iota(jnp.int32, (T, LANES), 1) < HEAD_DIM

    for g in range(SSD_GROUPS):
        b_off = SSD_WIDTH + g * SSD_STATE
        c_off = SSD_WIDTH + SSD_GROUPS * SSD_STATE + g * SSD_STATE
        b_g = xc_ref[rows, b_off:b_off + SSD_STATE]
        c_g = xc_ref[rows, c_off:c_off + SSD_STATE]
        cb = lax.dot_general(c_g.astype(BF16), b_g.astype(BF16), (((1,), (1,)), ((), ())),
                             preferred_element_type=F32)
        b_t = b_g.T
        s_prev = state_ref[g]
        y_off = jnp.dot(c_g.astype(BF16), s_prev.astype(BF16), preferred_element_type=F32)
        for j in range(4):
            h0 = g * 8 + 2 * j
            slab = g * 4 + j
            xs_pair = xc_ref[rows, slab * LANES:(slab + 1) * LANES]
            rhs = jnp.concatenate([jnp.where(lane_lo, xs_pair, 0.0),
                                   jnp.where(lane_lo, 0.0, xs_pair)], axis=0).astype(BF16)
            m_parts, s_parts = [], []
            for hh in (h0, h0 + 1):
                diff = acol2[:, hh:hh + 1] - arow2[hh:hh + 1, :]
                m_parts.append(cb * jnp.exp2(jnp.where(causal, diff, -jnp.inf)))
                s_parts.append(b_t * wst_t[hh:hh + 1, :])
            lhs_y = jnp.concatenate(m_parts, axis=1).astype(BF16)
            lhs_s = jnp.concatenate(s_parts, axis=1).astype(BF16)
            y_diag = jnp.dot(lhs_y, rhs, preferred_element_type=F32)
            st_new = jnp.dot(lhs_s, rhs, preferred_element_type=F32)
            off_scale = jnp.where(lane_lo, e_acum[:, h0:h0 + 1], e_acum[:, h0 + 1:h0 + 2])
            y_pair = (y_diag + off_scale * y_off[:, j * LANES:(j + 1) * LANES]
                      + dskip_ref[:, slab * LANES:(slab + 1) * LANES] * xs_pair)
            ybuf_ref[rows, slab * LANES:(slab + 1) * LANES] = y_pair
            cd = jnp.where(lane_lo[0:1, :], e_total[:, h0:h0 + 1], e_total[:, h0 + 1:h0 + 2])
            state_ref[g, :, j * LANES:(j + 1) * LANES] = s_prev[:, j * LANES:(j + 1) * LANES] * cd + st_new

    gw = SSD_WIDTH // SSD_GROUPS
    for g in range(SSD_GROUPS):
        hgate = ybuf_ref[rows, g * gw:(g + 1) * gw] * z_ref[rows, g * gw:(g + 1) * gw]
        ms = jnp.mean(hgate * hgate, axis=-1, keepdims=True)
        y_ref[rows, g * gw:(g + 1) * gw] = (hgate * lax.rsqrt(ms + RMS_EPS)
                                            * normw_ref[:, g * gw:(g + 1) * gw]).astype(BF16)


def _attn_chunk(rows, k_prev, v_prev, first, sinks_ref, q_ref, k_ref, v_ref, y_ref, bias_ref):
    T = CHUNK
    kk = jnp.concatenate([k_prev, k_ref[rows, :]], axis=0).astype(F32)
    vv = jnp.concatenate([v_prev, v_ref[rows, :]], axis=0).astype(F32)
    kk_sw = pltpu.roll(kk, HEAD_DIM, axis=1)
    vv_sw = pltpu.roll(vv, HEAD_DIM, axis=1)
    lane_lo = lax.broadcasted_iota(jnp.int32, (T, LANES), 1) < HEAD_DIM
    lane2_lo = lax.broadcasted_iota(jnp.int32, (2 * T, LANES), 1) < HEAD_DIM
    row4 = lax.broadcasted_iota(jnp.int32, (4 * T, LANES), 0)
    lane4 = lax.broadcasted_iota(jnp.int32, (4 * T, LANES), 1)
    ones_cols = jnp.where((row4 < 2 * T) == (lane4 < HEAD_DIM), 1.0, 0.0).astype(BF16)

    for g in range(ATTN_KV_HEADS):
        own, other = (kk, kk_sw) if g == 0 else (kk_sw, kk)
        kk2 = jnp.concatenate([jnp.where(lane2_lo, own, 0.0), jnp.where(lane2_lo, 0.0, other)],
                              axis=0).astype(BF16)
        own, other = (vv, vv_sw) if g == 0 else (vv_sw, vv)
        vv2 = jnp.concatenate([jnp.where(lane2_lo, own, 0.0), jnp.where(lane2_lo, 0.0, other)],
                              axis=0).astype(BF16)
        vv2 = jnp.concatenate([vv2, ones_cols], axis=1)
        qs = jnp.concatenate([q_ref[rows, (g * 4 + j) * LANES:(g * 4 + j + 1) * LANES] for j in range(4)], axis=0)
        s_all = lax.dot_general(qs, kk2, (((1,), (1,)), ((), ())), preferred_element_type=F32)
        p_rows, sink_rows = [], []
        for j in range(4):
            p_halves, sink_halves = [], []
            for half in range(2):
                hh = g * 8 + 2 * j + half
                s = s_all[j * T:(j + 1) * T, half * 2 * T:(half + 1) * 2 * T] + bias_ref[first, hh]
                sink2 = sinks_ref[hh] * LOG2E
                m = jnp.maximum(jnp.max(s, axis=-1, keepdims=True), sink2)
                p_halves.append(jnp.exp2(s - m).astype(BF16))
                sink_halves.append(jnp.exp2(sink2 - m))
            p_rows.append(jnp.concatenate(p_halves, axis=1))
            sink_rows.append(jnp.where(lane_lo, sink_halves[0], sink_halves[1]))
        p_all = jnp.concatenate(p_rows, axis=0)
        o_all = jnp.dot(p_all, vv2, preferred_element_type=F32)
        for j in range(4):
            col0 = SSD_WIDTH + (g * 4 + j) * LANES
            o_j = o_all[j * T:(j + 1) * T, :]
            den = o_j[:, LANES:] + sink_rows[j]
            y_ref[rows, col0:col0 + LANES] = (o_j[:, :LANES] * (1.0 / den)).astype(BF16)


def _init_attn_bias(bias_ref):
    T = CHUNK
    qi = lax.broadcasted_iota(jnp.int32, (T, 2 * T), 0)
    kj = lax.broadcasted_iota(jnp.int32, (T, 2 * T), 1)
    dist = qi + T - kj
    band = (dist >= 0) & (dist < T)
    dist_f = dist.astype(F32)
    for hh in range(ATTN_HEADS):
        alibi = dist_f * (-SLOPES[hh] * LOG2E)
        bias_ref[0, hh] = jnp.where(band, alibi, -jnp.inf)
        bias_ref[1, hh] = jnp.where(band & (kj >= T), alibi, -jnp.inf)


def _mixer_kernel(sinks_ref, z_ref, xbc_ref, dt_ref, q_ref, k_ref, v_ref,
                  dtb_ref, alog_ref, dskip_ref, normw_ref,
                  y_ref,
                  state_ref, kprev_ref, vprev_ref, ybuf_ref, bias_ref):
    t = pl.program_id(1)
    T = CHUNK
    n_chunks = z_ref.shape[0] // T

    @pl.when(t == 0)
    def _():
        state_ref[...] = jnp.zeros_like(state_ref)
        kprev_ref[...] = jnp.zeros_like(kprev_ref)
        vprev_ref[...] = jnp.zeros_like(vprev_ref)

    @pl.when((pl.program_id(0) == 0) & (t == 0))
    def _():
        _init_attn_bias(bias_ref)

    for ci in range(n_chunks):
        rows = pl.ds(ci * T, T)
        _ssd_chunk(rows, xbc_ref, z_ref, dt_ref, dtb_ref, alog_ref, dskip_ref, normw_ref, y_ref, state_ref,
                   ybuf_ref)
        if ci == 0:
            k_prev, v_prev = kprev_ref[...], vprev_ref[...]
            first = jnp.where(t == 0, 1, 0)
        else:
            prev_rows = pl.ds((ci - 1) * T, T)
            k_prev, v_prev = k_ref[prev_rows, :], v_ref[prev_rows, :]
            first = 0
        _attn_chunk(rows, k_prev, v_prev, first, sinks_ref, q_ref, k_ref, v_ref, y_ref, bias_ref)

    last_rows = pl.ds((n_chunks - 1) * T, T)
    kprev_ref[...] = k_ref[last_rows, :]
    vprev_ref[...] = v_ref[last_rows, :]


def _mixer(sinks, z, xbc, dt, q, k, v, dtb, alog, dskip, normw, batch, seq, tm):
    tiles_per_seq = seq // tm
    n_tok = batch * seq
    row = lambda b, t: (b * tiles_per_seq + t, 0)
    const = lambda b, t: (0, 0)
    vspec = lambda w: pl.BlockSpec((tm, w), row)
    pspec = lambda r, w: pl.BlockSpec((r, w), const)
    return pl.pallas_call(
        _mixer_kernel,
        grid=(batch, tiles_per_seq),
        in_specs=[pl.BlockSpec(memory_space=pltpu.SMEM),
                  vspec(SSD_WIDTH), vspec(CONV_DIM), vspec(DT_PAD), vspec(ATTN_WIDTH), vspec(KV_WIDTH),
                  vspec(KV_WIDTH),
                  pspec(1, DT_PAD), pspec(1, DT_PAD),
                  pspec(1, SSD_WIDTH), pspec(1, SSD_WIDTH)],
        out_specs=pl.BlockSpec((tm, MIX_WIDTH), row),
        out_shape=jax.ShapeDtypeStruct((n_tok, MIX_WIDTH), BF16),
        scratch_shapes=[pltpu.VMEM((SSD_GROUPS, SSD_STATE, SSD_WIDTH // SSD_GROUPS), F32),
                        pltpu.VMEM((CHUNK, KV_WIDTH), BF16),
                        pltpu.VMEM((CHUNK, KV_WIDTH), BF16),
                        pltpu.VMEM((tm, SSD_WIDTH), F32),
                        pltpu.VMEM((2, ATTN_HEADS, CHUNK, 2 * CHUNK), F32)],
        compiler_params=pltpu.CompilerParams(dimension_semantics=("arbitrary", "arbitrary"),
                                             vmem_limit_bytes=VMEM_LIMIT),
        name="mixer",
    )(sinks, z, xbc, dt, q, k, v, dtb, alog, dskip, normw)


FF_CHUNK = 1024


def _out_ffn_kernel(x_ref, y_ref, mod_ref, lng_ref, lnb_ref, wout_ref, ln1g_ref, ln1b_ref,
                    w1_ref, b1_ref, w2_ref, b2_ref, ln2g_ref, ln2b_ref, o_ref):
    h0 = _layer_norm(x_ref[...], lng_ref[...], lnb_ref[...])
    mix = jnp.dot(y_ref[...], wout_ref[...], preferred_element_type=F32)
    g1 = mod_ref[:, 2 * D_MODEL:3 * D_MODEL]
    sh2 = mod_ref[:, 3 * D_MODEL:4 * D_MODEL]
    sc2 = mod_ref[:, 4 * D_MODEL:5 * D_MODEL]
    g2 = mod_ref[:, 5 * D_MODEL:6 * D_MODEL]
    h1 = _layer_norm(ALPHA * h0 + (1.0 + g1) * mix, ln1g_ref[...], ln1b_ref[...])
    u = (h1 * (1.0 + sc2) + sh2).astype(BF16)
    f = jnp.zeros_like(h1)
    for ck in range(D_FF // FF_CHUNK):
        sl = slice(ck * FF_CHUNK, (ck + 1) * FF_CHUNK)
        hid = jnp.dot(u, w1_ref[:, sl], preferred_element_type=F32) + b1_ref[:, sl]
        hid = jnp.square(jnp.maximum(hid, 0.0)).astype(BF16)
        f = f + jnp.dot(hid, w2_ref[sl, :], preferred_element_type=F32)
    f = f + b2_ref[...]
    o_ref[...] = _layer_norm(ALPHA * h1 + (1.0 + g2) * f, ln2g_ref[...], ln2b_ref[...])


def _out_ffn(x2, ycat, mod3, ln_g, ln_b, w_out, ln1_g, ln1_b, w1, b1, w2, b2, ln2_g, ln2_b, seq, tm):
    n_tok = x2.shape[0]
    tiles_per_seq = seq // tm
    row = lambda i: (i, 0)
    const = lambda i: (0, 0)
    resident = lambda r, w: pl.BlockSpec((r, w), const, pipeline_mode=pl.Buffered(1))
    return pl.pallas_call(
        _out_ffn_kernel,
        grid=(n_tok // tm,),
        in_specs=[pl.BlockSpec((tm, D_MODEL), row),
                  pl.BlockSpec((tm, MIX_WIDTH), row),
                  pl.BlockSpec((None, 1, 6 * D_MODEL), lambda i: (i // tiles_per_seq, 0, 0)),
                  resident(1, D_MODEL), resident(1, D_MODEL),
                  resident(MIX_WIDTH, D_MODEL),
                  resident(1, D_MODEL), resident(1, D_MODEL),
                  resident(D_MODEL, D_FF), resident(1, D_FF),
                  resident(D_FF, D_MODEL), resident(1, D_MODEL),
                  resident(1, D_MODEL), resident(1, D_MODEL)],
        out_specs=pl.BlockSpec((tm, D_MODEL), row),
        out_shape=jax.ShapeDtypeStruct((n_tok, D_MODEL), F32),
        compiler_params=pltpu.CompilerParams(dimension_semantics=("arbitrary",),
                                             vmem_limit_bytes=VMEM_LIMIT),
        name="out_ffn",
    )(x2, ycat, mod3, ln_g, ln_b, w_out, ln1_g, ln1_b, w1, b1, w2, b2, ln2_g, ln2_b)


def kernel(x, c, ln_in_g, ln_in_b, ada_w, ada_b, w_in, conv_w, conv_b, dt_bias, a_log, d_skip, ssd_norm_w,
           attn_sinks, w_out, ln1_g, ln1_b, w_ff1, b_ff1, w_ff2, b_ff2, ln2_g, ln2_b):
    batch, seq, _ = x.shape
    depth = ada_w.shape[0]
    assert depth == 1 and seq % CHUNK == 0
    n_tok = batch * seq
    row2 = lambda p: p.reshape(1, -1)
    x2 = x.reshape(n_tok, D_MODEL)
    ln_g, ln_b = row2(ln_in_g), row2(ln_in_b)

    c_pad = jnp.pad(c, ((0, 8 - batch), (0, 0)))
    mod = _adaln_mod(c_pad, ada_w[0], row2(ada_b[0]))
    mod3 = mod[:batch].reshape(batch, 1, 6 * D_MODEL)

    wi = w_in[0]
    i1 = SSD_WIDTH
    i2 = i1 + CONV_DIM
    i3 = i2 + SSD_HEADS
    w_all = jnp.concatenate([wi[:, :i2], wi[:, i3:], wi[:, i2:i3],
                             jnp.zeros((D_MODEL, DT_PAD - SSD_HEADS), wi.dtype)], axis=1).astype(BF16)

    z, xbc, q, k, v, dt = _in_proj(x2, mod3, ln_g, ln_b, w_all, conv_w[0], row2(conv_b[0]), seq, tm=512)

    pad_heads = lambda p: jnp.pad(p, (0, DT_PAD - SSD_HEADS)).reshape(1, DT_PAD)
    ycat = _mixer(attn_sinks[0], z, xbc, dt, q, k, v,
                  pad_heads(dt_bias[0]), pad_heads(a_log[0]),
                  jnp.repeat(d_skip[0], HEAD_DIM).reshape(1, SSD_WIDTH), row2(ssd_norm_w[0]), batch, seq, tm=512)

    out = _out_ffn(x2, ycat, mod3, ln_g, ln_b, w_out[0].astype(BF16), row2(ln1_g[0]), row2(ln1_b[0]),
                   w_ff1[0].astype(BF16), row2(b_ff1[0]), w_ff2[0].astype(BF16), row2(b_ff2[0]),
                   row2(ln2_g[0]), row2(ln2_b[0]), seq, tm=512)
    return out.reshape(batch, seq, D_MODEL)
```

```python
import functools
import math

import numpy as np
import jax
import jax.numpy as jnp
from jax import lax
from jax.experimental import pallas as pl
from jax.experimental.pallas import tpu as pltpu

D_MODEL = 1024
HEAD_DIM = 64
SSD_WIDTH = 1024
SSD_HEADS = 16
SSD_GROUPS = 2
SSD_STATE = 128
CHUNK = 128
CONV_K = 4
CONV_DIM = SSD_WIDTH + 2 * SSD_GROUPS * SSD_STATE
ATTN_WIDTH = 1024
ATTN_HEADS = 16
ATTN_KV_HEADS = 2
KV_WIDTH = ATTN_KV_HEADS * HEAD_DIM
D_FF = 4096
MIX_WIDTH = SSD_WIDTH + ATTN_WIDTH
LANES = 128
DT_PAD = LANES
PROJ_PAD = SSD_WIDTH + CONV_DIM + ATTN_WIDTH + 2 * KV_WIDTH + DT_PAD
ALPHA = 2.0 ** 0.25
LOG2E = math.log2(math.e)
Q_SCALE = HEAD_DIM ** -0.5 * LOG2E
LN_EPS = 1e-5
RMS_EPS = 1e-5
VMEM_LIMIT = 56 * 1024 * 1024
INPROJ_FLAGS = {}

F32 = jnp.float32
BF16 = jnp.bfloat16


def _alibi_slopes(n):
    start = 2.0 ** (-8.0 / n)
    return [float(np.float32(start ** (i + 1))) for i in range(n)]


SLOPES = _alibi_slopes(ATTN_HEADS)


def _layer_norm(x, g, b):
    mu = jnp.mean(x, axis=-1, keepdims=True)
    xc = x - mu
    var = jnp.mean(xc * xc, axis=-1, keepdims=True)
    return xc * lax.rsqrt(var + LN_EPS) * g + b


def _silu(x):
    return x * (1.0 / (1.0 + jnp.exp(-x)))


def _adaln_kernel(c_ref, w_ref, b_ref, o_ref):
    cs = _silu(c_ref[...]).astype(BF16)
    o_ref[...] = jnp.dot(cs, w_ref[...].astype(BF16), preferred_element_type=F32) + b_ref[...]


def _adaln_mod(c_pad, ada_w, ada_b):
    rows, n = c_pad.shape[0], ada_w.shape[1]
    tn = 1024
    return pl.pallas_call(
        _adaln_kernel,
        grid=(n // tn,),
        in_specs=[pl.BlockSpec((rows, D_MODEL), lambda j: (0, 0)),
                  pl.BlockSpec((D_MODEL, tn), lambda j: (0, j)),
                  pl.BlockSpec((1, tn), lambda j: (0, j))],
        out_specs=pl.BlockSpec((rows, tn), lambda j: (0, j)),
        out_shape=jax.ShapeDtypeStruct((rows, n), F32),
        compiler_params=pltpu.CompilerParams(dimension_semantics=("arbitrary",),
                                             vmem_limit_bytes=VMEM_LIMIT),
        name="adaln_mod",
    )(c_pad, ada_w, ada_b)


_OFF_Z = 0
_OFF_XBC = _OFF_Z + SSD_WIDTH
_OFF_Q = _OFF_XBC + CONV_DIM
_OFF_K = _OFF_Q + ATTN_WIDTH
_OFF_V = _OFF_K + KV_WIDTH
_OFF_DT = _OFF_V + KV_WIDTH
MM_BLOCK = 256
N_CONV_BLOCKS = CONV_DIM // MM_BLOCK


def _inproj_body(x_ref, mod_ref, g_ref, b_ref, w_ref, convw_ref, convb_ref,
                 z_ref, xbc_ref, q_ref, k_ref, v_ref, dt_ref, xpad_refs):
    tm = x_ref.shape[0]
    sh1 = mod_ref[:, 0:D_MODEL]
    sc1 = mod_ref[:, D_MODEL:2 * D_MODEL]
    gain = g_ref[...] * (1.0 + sc1)
    shift = b_ref[...] * (1.0 + sc1) + sh1
    x = x_ref[...]
    mu = jnp.mean(x, axis=-1, keepdims=True)
    xc = x - mu
    var = jnp.mean(xc * xc, axis=-1, keepdims=True)
    u = (xc * lax.rsqrt(var + LN_EPS) * gain + shift).astype(BF16)
    yield

    def proj(off, width):
        return jnp.dot(u, w_ref[:, off:off + width], preferred_element_type=F32)

    def conv_block(j):
        xpad_ref = xpad_refs[j]
        cols = slice(j * MM_BLOCK, (j + 1) * MM_BLOCK)
        conv = convb_ref[:, cols] + convw_ref[3:4, cols] * xpad_ref[8:8 + tm, :]
        for kk in range(CONV_K - 1):
            conv = conv + convw_ref[kk:kk + 1, cols] * xpad_ref[5 + kk:5 + kk + tm, :]
        xpad_ref[0:8, :] = xpad_ref[tm:tm + 8, :]
        xbc_ref[:, cols] = _silu(conv)

    def z_block(j):
        cols = slice(j * MM_BLOCK, (j + 1) * MM_BLOCK)
        z_ref[:, cols] = _silu(proj(_OFF_Z + j * MM_BLOCK, MM_BLOCK))

    def q_block(j):
        cols = slice(j * MM_BLOCK, (j + 1) * MM_BLOCK)
        q_ref[:, cols] = (proj(_OFF_Q + j * MM_BLOCK, MM_BLOCK) * Q_SCALE).astype(BF16)

    fillers = ([functools.partial(z_block, j) for j in range(SSD_WIDTH // MM_BLOCK)]
               + [functools.partial(q_block, j) for j in range(ATTN_WIDTH // MM_BLOCK)])
    n_conv = len(xpad_refs)
    xpad_refs[0][8:8 + tm, :] = proj(_OFF_XBC, MM_BLOCK)
    yield
    for j in range(n_conv):
        if j + 1 < n_conv:
            xpad_refs[j + 1][8:8 + tm, :] = proj(_OFF_XBC + (j + 1) * MM_BLOCK, MM_BLOCK)
        if fillers:
            fillers.pop(0)()
        conv_block(j)
        yield
    for fill in fillers:
        fill()
        yield
    k_ref[...] = proj(_OFF_K, KV_WIDTH).astype(BF16)
    v_ref[...] = proj(_OFF_V, KV_WIDTH).astype(BF16)
    dt_ref[...] = proj(_OFF_DT, DT_PAD)


def _ssd_chunk(rows, xc_ref, z_ref, dt_ref, dtb_ref, alog_ref, dskip_ref, normw_ref, y_ref, state_ref, ybuf_ref):
    T = CHUNK
    dt_in = dt_ref[rows, :] + dtb_ref[...]
    dt = jnp.maximum(dt_in, 0.0) + jnp.log(1.0 + jnp.exp(-jnp.abs(dt_in)))
    a = dt * (-jnp.exp(alog_ref[...]))
    row = lax.broadcasted_iota(jnp.int32, (T, T), 0)
    col = lax.broadcasted_iota(jnp.int32, (T, T), 1)
    causal = row >= col
    tril = jnp.where(causal, 1.0, 0.0).astype(F32)
    acum = jnp.dot(tril, a, preferred_element_type=F32, precision=lax.Precision.HIGHEST)
    total = acum[T - 1:T, :]
    acol2 = acum * LOG2E
    arow2 = ((acum - jnp.log(dt)) * LOG2E).T
    wst_t = (jnp.exp(total - acum) * dt).T
    e_acum = jnp.exp(acum)
    e_total = jnp.exp(total)
    lane_lo = lax.broadcasted_iota(jnp.int32, (T, LANES), 1) < HEAD_DIM
    yield

    for g in range(SSD_GROUPS):
        b_off = SSD_WIDTH + g * SSD_STATE
        c_off = SSD_WIDTH + SSD_GROUPS * SSD_STATE + g * SSD_STATE
        b_g = xc_ref[rows, b_off:b_off + SSD_STATE]
        c_g = xc_ref[rows, c_off:c_off + SSD_STATE]
        cb = lax.dot_general(c_g.astype(BF16), b_g.astype(BF16), (((1,), (1,)), ((), ())),
                             preferred_element_type=F32)
        b_t = b_g.T
        s_prev = state_ref[g]
        y_off = jnp.dot(c_g.astype(BF16), s_prev.astype(BF16), preferred_element_type=F32)
        for j in range(4):
            h0 = g * 8 + 2 * j
            slab = g * 4 + j
            xs_pair = xc_ref[rows, slab * LANES:(slab + 1) * LANES]
            rhs = jnp.concatenate([jnp.where(lane_lo, xs_pair, 0.0),
                                   jnp.where(lane_lo, 0.0, xs_pair)], axis=0).astype(BF16)
            m_parts, s_parts = [], []
            for hh in (h0, h0 + 1):
                diff = acol2[:, hh:hh + 1] - arow2[hh:hh + 1, :]
                m_parts.append(cb * jnp.exp2(jnp.where(causal, diff, -jnp.inf)))
                s_parts.append(b_t * wst_t[hh:hh + 1, :])
            lhs_y = jnp.concatenate(m_parts, axis=1).astype(BF16)
            lhs_s = jnp.concatenate(s_parts, axis=1).astype(BF16)
            y_diag = jnp.dot(lhs_y, rhs, preferred_element_type=F32)
            st_new = jnp.dot(lhs_s, rhs, preferred_element_type=F32)
            off_scale = jnp.where(lane_lo, e_acum[:, h0:h0 + 1], e_acum[:, h0 + 1:h0 + 2])
            y_pair = (y_diag + off_scale * y_off[:, j * LANES:(j + 1) * LANES]
                      + dskip_ref[:, slab * LANES:(slab + 1) * LANES] * xs_pair)
            ybuf_ref[rows, slab * LANES:(slab + 1) * LANES] = y_pair
            cd = jnp.where(lane_lo[0:1, :], e_total[:, h0:h0 + 1], e_total[:, h0 + 1:h0 + 2])
            state_ref[g, :, j * LANES:(j + 1) * LANES] = s_prev[:, j * LANES:(j + 1) * LANES] * cd + st_new
        yield

    gw = SSD_WIDTH // SSD_GROUPS
    for g in range(SSD_GROUPS):
        hgate = ybuf_ref[rows, g * gw:(g + 1) * gw] * z_ref[rows, g * gw:(g + 1) * gw]
        ms = jnp.mean(hgate * hgate, axis=-1, keepdims=True)
        y_ref[rows, g * gw:(g + 1) * gw] = (hgate * lax.rsqrt(ms + RMS_EPS)
                                            * normw_ref[:, g * gw:(g + 1) * gw]).astype(BF16)


def _attn_chunk(rows, k_prev, v_prev, first, sinks_ref, q_ref, k_ref, v_ref, y_ref, bias_ref):
    T = CHUNK
    kk = jnp.concatenate([k_prev, k_ref[rows, :]], axis=0).astype(F32)
    vv = jnp.concatenate([v_prev, v_ref[rows, :]], axis=0).astype(F32)
    kk_sw = pltpu.roll(kk, HEAD_DIM, axis=1)
    vv_sw = pltpu.roll(vv, HEAD_DIM, axis=1)
    lane_lo = lax.broadcasted_iota(jnp.int32, (T, LANES), 1) < HEAD_DIM
    lane2_lo = lax.broadcasted_iota(jnp.int32, (2 * T, LANES), 1) < HEAD_DIM
    row4 = lax.broadcasted_iota(jnp.int32, (4 * T, LANES), 0)
    lane4 = lax.broadcasted_iota(jnp.int32, (4 * T, LANES), 1)
    ones_cols = jnp.where((row4 < 2 * T) == (lane4 < HEAD_DIM), 1.0, 0.0).astype(BF16)

    for g in range(ATTN_KV_HEADS):
        own, other = (kk, kk_sw) if g == 0 else (kk_sw, kk)
        kk2 = jnp.concatenate([jnp.where(lane2_lo, own, 0.0), jnp.where(lane2_lo, 0.0, other)],
                              axis=0).astype(BF16)
        own, other = (vv, vv_sw) if g == 0 else (vv_sw, vv)
        vv2 = jnp.concatenate([jnp.where(lane2_lo, own, 0.0), jnp.where(lane2_lo, 0.0, other)],
                              axis=0).astype(BF16)
        vv2 = jnp.concatenate([vv2, ones_cols], axis=1)
        for j in range(4):
            col0 = (g * 4 + j) * LANES
            s_pair = lax.dot_general(q_ref[rows, col0:col0 + LANES], kk2, (((1,), (1,)), ((), ())),
                                     preferred_element_type=F32)
            p_halves, sink_halves = [], []
            for half in range(2):
                hh = g * 8 + 2 * j + half
                s = s_pair[:, half * 2 * T:(half + 1) * 2 * T] + bias_ref[first, hh]
                sink2 = sinks_ref[hh] * LOG2E
                m = jnp.maximum(jnp.max(s, axis=-1, keepdims=True), sink2)
                p_halves.append(jnp.exp2(s - m).astype(BF16))
                sink_halves.append(jnp.exp2(sink2 - m))
            o_j = jnp.dot(jnp.concatenate(p_halves, axis=1), vv2, preferred_element_type=F32)
            den = o_j[:, LANES:] + jnp.where(lane_lo, sink_halves[0], sink_halves[1])
            y_ref[rows, SSD_WIDTH + col0:SSD_WIDTH + col0 + LANES] = (o_j[:, :LANES] * (1.0 / den)).astype(BF16)
        yield


def _init_attn_bias(bias_ref):
    T = CHUNK
    qi = lax.broadcasted_iota(jnp.int32, (T, 2 * T), 0)
    kj = lax.broadcasted_iota(jnp.int32, (T, 2 * T), 1)
    dist = qi + T - kj
    band = (dist >= 0) & (dist < T)
    dist_f = dist.astype(F32)
    for hh in range(ATTN_HEADS):
        alibi = dist_f * (-SLOPES[hh] * LOG2E)
        bias_ref[0, hh] = jnp.where(band, alibi, -jnp.inf)
        bias_ref[1, hh] = jnp.where(band & (kj >= T), alibi, -jnp.inf)


def _mixer_body(seq_start, sinks_ref, z_ref, xbc_ref, dt_ref, q_ref, k_ref, v_ref,
                dtb_ref, alog_ref, dskip_ref, normw_ref, y_ref,
                state_ref, kprev_ref, vprev_ref, ybuf_ref, bias_ref):
    T = CHUNK
    for ci in range(y_ref.shape[0] // T):
        rows = pl.ds(ci * T, T)
        yield from _ssd_chunk(rows, xbc_ref, z_ref, dt_ref, dtb_ref, alog_ref, dskip_ref, normw_ref, y_ref,
                              state_ref, ybuf_ref)
        yield
        if ci == 0:
            k_prev, v_prev = kprev_ref[...], vprev_ref[...]
            first = jnp.where(seq_start, 1, 0)
        else:
            prev_rows = pl.ds((ci - 1) * T, T)
            k_prev, v_prev = k_ref[prev_rows, :], v_ref[prev_rows, :]
            first = 0
        yield from _attn_chunk(rows, k_prev, v_prev, first, sinks_ref, q_ref, k_ref, v_ref, y_ref, bias_ref)


_DONE = object()
_PROJ_PIECES = [(SSD_WIDTH, F32), (CONV_DIM, F32), (ATTN_WIDTH, BF16), (KV_WIDTH, BF16), (KV_WIDTH, BF16),
                (DT_PAD, F32)]


W_IN_STAGE = 256


def _load_w_in_bf16(wt_hbm, w_ref, stage_ref, sem_ref):
    i2 = SSD_WIDTH + CONV_DIM
    i3 = i2 + SSD_HEADS
    blocks = [(c, c) for c in range(0, i2, W_IN_STAGE)]
    blocks += [(i3 + c, i2 + c) for c in range(0, _OFF_DT - i2, W_IN_STAGE)]

    def copy(i):
        src = wt_hbm.at[0, pl.ds(blocks[i][0], W_IN_STAGE), :]
        return pltpu.make_async_copy(src, stage_ref.at[i % 2], sem_ref.at[i % 2])

    stage_ref[2, 0:DT_PAD, :] = jnp.zeros((DT_PAD, D_MODEL), F32)
    dt_copy = pltpu.make_async_copy(wt_hbm.at[0, pl.ds(i2, SSD_HEADS), :], stage_ref.at[2, pl.ds(0, SSD_HEADS), :],
                                    sem_ref.at[2])
    dt_copy.start()
    copy(0).start()
    for i in range(len(blocks)):
        if i + 1 < len(blocks):
            copy(i + 1).start()
        copy(i).wait()
        dst = blocks[i][1]
        w_ref[:, dst:dst + W_IN_STAGE] = stage_ref[i % 2].T.astype(BF16)
    dt_copy.wait()
    w_ref[:, _OFF_DT:_OFF_DT + DT_PAD] = stage_ref[2, 0:DT_PAD, :].T.astype(BF16)


def _proj_mixer_kernel(sinks_ref, x_ref, mod_ref, g_ref, b_ref, wt_hbm, convw_ref, convb_ref,
                       dtb_ref, alog_ref, dskip_ref, normw_ref, wout_ref, w1_ref, w2_ref,
                       y_ref, wout_bf_ref, w1_bf_ref, w2_bf_ref,
                       w_ref, wstage_ref, wsem_ref, *scratch, tiles_per_seq, n_tiles):
    staged = scratch[:len(_PROJ_PIECES)]
    xpad_refs = scratch[len(_PROJ_PIECES):len(_PROJ_PIECES) + N_CONV_BLOCKS]
    state_ref, kprev_ref, vprev_ref, ybuf_ref, bias_ref = scratch[len(_PROJ_PIECES) + N_CONV_BLOCKS:]
    s = pl.program_id(0)
    wr = s % 2
    rd = 1 - wr
    mix_seq_start = (s - 1) % tiles_per_seq == 0

    @pl.when(s == 0)
    def _():
        _load_w_in_bf16(wt_hbm, w_ref, wstage_ref, wsem_ref)
        _init_attn_bias(bias_ref)

    for src, dst in ((wout_ref, wout_bf_ref), (w1_ref, w1_bf_ref), (w2_ref, w2_bf_ref)):
        dst[...] = src[...].astype(BF16)

    @pl.when(s % tiles_per_seq == 0)
    def _():
        for xpad_ref in xpad_refs:
            xpad_ref[0:8, :] = jnp.zeros((8, MM_BLOCK), F32)

    @pl.when(mix_seq_start)
    def _():
        state_ref[...] = jnp.zeros_like(state_ref)
        kprev_ref[...] = jnp.zeros_like(kprev_ref)
        vprev_ref[...] = jnp.zeros_like(vprev_ref)

    def run(project, mix):
        works = []
        if project:
            works.append(_inproj_body(x_ref, mod_ref, g_ref, b_ref, w_ref, convw_ref, convb_ref,
                                      *[ref.at[wr] for ref in staged], xpad_refs))
        if mix:
            z_s, xbc_s, q_s, k_s, v_s, dt_s = [ref.at[rd] for ref in staged]
            works.append(_mixer_body(mix_seq_start, sinks_ref, z_s, xbc_s, dt_s, q_s, k_s, v_s,
                                     dtb_ref, alog_ref, dskip_ref, normw_ref, y_ref,
                                     state_ref, kprev_ref, vprev_ref, ybuf_ref, bias_ref))
        while works:
            for work in list(works):
                if next(work, _DONE) is _DONE:
                    works.remove(work)
        if mix:
            last_rows = pl.ds(y_ref.shape[0] - CHUNK, CHUNK)
            kprev_ref[...] = k_s[last_rows, :]
            vprev_ref[...] = v_s[last_rows, :]

    pl.when(s == 0)(lambda: run(project=True, mix=False))
    pl.when(s > 0)(lambda: run(project=True, mix=True))


def _proj_mixer(sinks, x2, mod3, ln_g, ln_b, w_in_t, conv_w, conv_b, dtb, alog, dskip, normw, next_weights, seq, tm):
    n_tok = x2.shape[0]
    n_tiles = n_tok // tm
    tiles_per_seq = seq // tm
    const = lambda s: (0, 0)
    proj_tile = lambda s: jnp.minimum(s, n_tiles - 1)
    pspec = lambda r, w: pl.BlockSpec((r, w), const)
    wspec = lambda w: pl.BlockSpec((w.shape[0] // n_tiles, w.shape[1]), lambda s: (proj_tile(s), 0))
    return pl.pallas_call(
        functools.partial(_proj_mixer_kernel, tiles_per_seq=tiles_per_seq, n_tiles=n_tiles),
        grid=(n_tiles + 1,),
        in_specs=[pl.BlockSpec(memory_space=pltpu.SMEM),
                  pl.BlockSpec((tm, D_MODEL), lambda s: (proj_tile(s), 0)),
                  pl.BlockSpec((None, 1, 6 * D_MODEL), lambda s: (proj_tile(s) // tiles_per_seq, 0, 0)),
                  pspec(1, D_MODEL), pspec(1, D_MODEL),
                  pl.BlockSpec(memory_space=pl.ANY),
                  pspec(CONV_K, CONV_DIM), pspec(1, CONV_DIM),
                  pspec(1, DT_PAD), pspec(1, DT_PAD), pspec(1, SSD_WIDTH), pspec(1, SSD_WIDTH)]
                 + [wspec(w) for w in next_weights],
        out_specs=[pl.BlockSpec((tm, MIX_WIDTH), lambda s: (jnp.maximum(s - 1, 0), 0))]
                  + [wspec(w) for w in next_weights],
        out_shape=[jax.ShapeDtypeStruct((n_tok, MIX_WIDTH), BF16)]
                  + [jax.ShapeDtypeStruct(w.shape, BF16) for w in next_weights],
        scratch_shapes=([pltpu.VMEM((D_MODEL, PROJ_PAD), BF16),
                         pltpu.VMEM((3, W_IN_STAGE, D_MODEL), F32),
                         pltpu.SemaphoreType.DMA((3,))]
                        + [pltpu.VMEM((2, tm, w), dt) for w, dt in _PROJ_PIECES]
                        + [pltpu.VMEM((tm + 8, MM_BLOCK), F32)] * N_CONV_BLOCKS
                        + [pltpu.VMEM((SSD_GROUPS, SSD_STATE, SSD_WIDTH // SSD_GROUPS), F32),
                           pltpu.VMEM((CHUNK, KV_WIDTH), BF16),
                           pltpu.VMEM((CHUNK, KV_WIDTH), BF16),
                           pltpu.VMEM((tm, SSD_WIDTH), F32),
                           pltpu.VMEM((2, ATTN_HEADS, CHUNK, 2 * CHUNK), F32)]),
        compiler_params=pltpu.CompilerParams(dimension_semantics=("arbitrary",),
                                             vmem_limit_bytes=VMEM_LIMIT, flags=INPROJ_FLAGS),
        name="proj_mixer",
    )(sinks, x2, mod3, ln_g, ln_b, w_in_t, conv_w, conv_b, dtb, alog, dskip, normw, *next_weights)


FF_CHUNK = 1024


def _out_ffn_rows(rows, x_ref, y_ref, mod_ref, lng_ref, lnb_ref, wout_ref, ln1g_ref, ln1b_ref,
                  w1_ref, b1_ref, w2_ref, b2_ref, ln2g_ref, ln2b_ref, o_ref):
    h0 = _layer_norm(x_ref[rows, :], lng_ref[...], lnb_ref[...])
    mix = jnp.dot(y_ref[rows, :], wout_ref[...], preferred_element_type=F32)
    yield
    g1 = mod_ref[:, 2 * D_MODEL:3 * D_MODEL]
    sh2 = mod_ref[:, 3 * D_MODEL:4 * D_MODEL]
    sc2 = mod_ref[:, 4 * D_MODEL:5 * D_MODEL]
    g2 = mod_ref[:, 5 * D_MODEL:6 * D_MODEL]
    h1 = _layer_norm(ALPHA * h0 + (1.0 + g1) * mix, ln1g_ref[...], ln1b_ref[...])
    u = (h1 * (1.0 + sc2) + sh2).astype(BF16)
    yield
    f = jnp.zeros_like(h1)
    for ck in range(D_FF // FF_CHUNK):
        sl = slice(ck * FF_CHUNK, (ck + 1) * FF_CHUNK)
        hid = jnp.dot(u, w1_ref[:, sl], preferred_element_type=F32) + b1_ref[:, sl]
        hid = jnp.square(jnp.maximum(hid, 0.0)).astype(BF16)
        f = f + jnp.dot(hid, w2_ref[sl, :], preferred_element_type=F32)
        yield
    f = f + b2_ref[...]
    o_ref[rows, :] = _layer_norm(ALPHA * h1 + (1.0 + g2) * f, ln2g_ref[...], ln2b_ref[...])


FFN_ROW_GROUPS = 2


def _out_ffn_kernel(x_ref, *refs):
    rows = x_ref.shape[0] // FFN_ROW_GROUPS
    waiting = [_out_ffn_rows(pl.ds(g * rows, rows), x_ref, *refs) for g in range(FFN_ROW_GROUPS)]
    running = []
    while waiting or running:
        if waiting:
            running.insert(0, waiting.pop(0))
        for work in list(running):
            if next(work, _DONE) is _DONE:
                running.remove(work)


def _out_ffn(x2, ycat, mod3, ln_g, ln_b, w_out, ln1_g, ln1_b, w1, b1, w2, b2, ln2_g, ln2_b, seq, tm):
    n_tok = x2.shape[0]
    tiles_per_seq = seq // tm
    row = lambda i: (i, 0)
    const = lambda i: (0, 0)
    resident = lambda r, w: pl.BlockSpec((r, w), const, pipeline_mode=pl.Buffered(1))
    return pl.pallas_call(
        _out_ffn_kernel,
        grid=(n_tok // tm,),
        in_specs=[pl.BlockSpec((tm, D_MODEL), row),
                  pl.BlockSpec((tm, MIX_WIDTH), row),
                  pl.BlockSpec((None, 1, 6 * D_MODEL), lambda i: (i // tiles_per_seq, 0, 0)),
                  resident(1, D_MODEL), resident(1, D_MODEL),
                  resident(MIX_WIDTH, D_MODEL),
                  resident(1, D_MODEL), resident(1, D_MODEL),
                  resident(D_MODEL, D_FF), resident(1, D_FF),
                  resident(D_FF, D_MODEL), resident(1, D_MODEL),
                  resident(1, D_MODEL), resident(1, D_MODEL)],
        out_specs=pl.BlockSpec((tm, D_MODEL), row),
        out_shape=jax.ShapeDtypeStruct((n_tok, D_MODEL), F32),
        compiler_params=pltpu.CompilerParams(dimension_semantics=("arbitrary",),
                                             vmem_limit_bytes=VMEM_LIMIT),
        name="out_ffn",
    )(x2, ycat, mod3, ln_g, ln_b, w_out, ln1_g, ln1_b, w1, b1, w2, b2, ln2_g, ln2_b)


def kernel(x, c, ln_in_g, ln_in_b, ada_w, ada_b, w_in, conv_w, conv_b, dt_bias, a_log, d_skip, ssd_norm_w,
           attn_sinks, w_out, ln1_g, ln1_b, w_ff1, b_ff1, w_ff2, b_ff2, ln2_g, ln2_b):
    batch, seq, _ = x.shape
    depth = ada_w.shape[0]
    assert depth == 1 and seq % CHUNK == 0
    n_tok = batch * seq
    row2 = lambda p: p.reshape(1, -1)
    x2 = x.reshape(n_tok, D_MODEL)
    ln_g, ln_b = row2(ln_in_g), row2(ln_in_b)

    c_pad = jnp.pad(c, ((0, 8 - batch), (0, 0)))
    mod = _adaln_mod(c_pad, ada_w[0], row2(ada_b[0]))
    mod3 = mod[:batch].reshape(batch, 1, 6 * D_MODEL)

    pad_heads = lambda p: jnp.pad(p, (0, DT_PAD - SSD_HEADS)).reshape(1, DT_PAD)
    ycat, w_out_bf, w_ff1_bf, w_ff2_bf = _proj_mixer(attn_sinks[0], x2, mod3, ln_g, ln_b, jnp.swapaxes(w_in, 1, 2), conv_w[0], row2(conv_b[0]),
                       pad_heads(dt_bias[0]), pad_heads(a_log[0]),
                       jnp.repeat(d_skip[0], HEAD_DIM).reshape(1, SSD_WIDTH), row2(ssd_norm_w[0]),
                       (w_out[0], w_ff1[0], w_ff2[0]), seq, tm=512)

    out = _out_ffn(x2, ycat, mod3, ln_g, ln_b, w_out_bf, row2(ln1_g[0]), row2(ln1_b[0]),
                   w_ff1_bf, row2(b_ff1[0]), w_ff2_bf, row2(b_ff2[0]),
                   row2(ln2_g[0]), row2(ln2_b[0]), seq, tm=512)
    return out.reshape(batch, seq, D_MODEL)
```

```python
import functools
import math

import numpy as np
import jax
import jax.numpy as jnp
from jax import lax
from jax.experimental import pallas as pl
from jax.experimental.pallas import tpu as pltpu

D_MODEL = 1024
HEAD_DIM = 64
SSD_WIDTH = 1024
SSD_HEADS = 16
SSD_GROUPS = 2
SSD_STATE = 128
CHUNK = 128
CONV_K = 4
CONV_DIM = SSD_WIDTH + 2 * SSD_GROUPS * SSD_STATE
ATTN_WIDTH = 1024
ATTN_HEADS = 16
ATTN_KV_HEADS = 2
KV_WIDTH = ATTN_KV_HEADS * HEAD_DIM
D_FF = 4096
MIX_WIDTH = SSD_WIDTH + ATTN_WIDTH
LANES = 128
DT_PAD = LANES
PROJ_PAD = SSD_WIDTH + CONV_DIM + ATTN_WIDTH + 2 * KV_WIDTH + DT_PAD
ALPHA = 2.0 ** 0.25
LOG2E = math.log2(math.e)
Q_SCALE = HEAD_DIM ** -0.5 * LOG2E
LN_EPS = 1e-5
RMS_EPS = 1e-5
VMEM_LIMIT = 56 * 1024 * 1024
INPROJ_FLAGS = {}

F32 = jnp.float32
BF16 = jnp.bfloat16


def _alibi_slopes(n):
    start = 2.0 ** (-8.0 / n)
    return [float(np.float32(start ** (i + 1))) for i in range(n)]


SLOPES = _alibi_slopes(ATTN_HEADS)


def _layer_norm(x, g, b):
    mu = jnp.mean(x, axis=-1, keepdims=True)
    xc = x - mu
    var = jnp.mean(xc * xc, axis=-1, keepdims=True)
    return xc * lax.rsqrt(var + LN_EPS) * g + b


def _silu(x):
    return x * (1.0 / (1.0 + jnp.exp(-x)))


def _adaln_kernel(c_ref, w_ref, b_ref, o_ref):
    cs = _silu(c_ref[...]).astype(BF16)
    o_ref[...] = jnp.dot(cs, w_ref[...].astype(BF16), preferred_element_type=F32) + b_ref[...]


def _adaln_mod(c_pad, ada_w, ada_b):
    rows, n = c_pad.shape[0], ada_w.shape[1]
    tn = 1024
    return pl.pallas_call(
        _adaln_kernel,
        grid=(n // tn,),
        in_specs=[pl.BlockSpec((rows, D_MODEL), lambda j: (0, 0)),
                  pl.BlockSpec((D_MODEL, tn), lambda j: (0, j)),
                  pl.BlockSpec((1, tn), lambda j: (0, j))],
        out_specs=pl.BlockSpec((rows, tn), lambda j: (0, j)),
        out_shape=jax.ShapeDtypeStruct((rows, n), F32),
        compiler_params=pltpu.CompilerParams(dimension_semantics=("arbitrary",),
                                             vmem_limit_bytes=VMEM_LIMIT),
        name="adaln_mod",
    )(c_pad, ada_w, ada_b)


_OFF_Z = 0
_OFF_XBC = _OFF_Z + SSD_WIDTH
_OFF_Q = _OFF_XBC + CONV_DIM
_OFF_K = _OFF_Q + ATTN_WIDTH
_OFF_V = _OFF_K + KV_WIDTH
_OFF_DT = _OFF_V + KV_WIDTH
MM_BLOCK = 256
N_CONV_BLOCKS = CONV_DIM // MM_BLOCK


def _inproj_body(x_ref, mod_ref, g_ref, b_ref, w_ref, convw_ref, convb_ref,
                 z_ref, xbc_ref, q_ref, k_ref, v_ref, dt_ref, xpad_refs):
    tm = x_ref.shape[0]
    sh1 = mod_ref[:, 0:D_MODEL]
    sc1 = mod_ref[:, D_MODEL:2 * D_MODEL]
    gain = g_ref[...] * (1.0 + sc1)
    shift = b_ref[...] * (1.0 + sc1) + sh1
    x = x_ref[...]
    mu = jnp.mean(x, axis=-1, keepdims=True)
    xc = x - mu
    var = jnp.mean(xc * xc, axis=-1, keepdims=True)
    u = (xc * lax.rsqrt(var + LN_EPS) * gain + shift).astype(BF16)
    yield

    def proj(off, width):
        return jnp.dot(u, w_ref[:, off:off + width], preferred_element_type=F32)

    def conv_block(j):
        xpad_ref = xpad_refs[j]
        cols = slice(j * MM_BLOCK, (j + 1) * MM_BLOCK)
        conv = convb_ref[:, cols] + convw_ref[3:4, cols] * xpad_ref[8:8 + tm, :]
        for kk in range(CONV_K - 1):
            conv = conv + convw_ref[kk:kk + 1, cols] * xpad_ref[5 + kk:5 + kk + tm, :]
        xpad_ref[0:8, :] = xpad_ref[tm:tm + 8, :]
        xbc_ref[:, cols] = _silu(conv)

    def z_block(j):
        cols = slice(j * MM_BLOCK, (j + 1) * MM_BLOCK)
        z_ref[:, cols] = _silu(proj(_OFF_Z + j * MM_BLOCK, MM_BLOCK))

    def q_block(j):
        cols = slice(j * MM_BLOCK, (j + 1) * MM_BLOCK)
        q_ref[:, cols] = (proj(_OFF_Q + j * MM_BLOCK, MM_BLOCK) * Q_SCALE).astype(BF16)

    fillers = ([functools.partial(z_block, j) for j in range(SSD_WIDTH // MM_BLOCK)]
               + [functools.partial(q_block, j) for j in range(ATTN_WIDTH // MM_BLOCK)])
    n_conv = len(xpad_refs)
    xpad_refs[0][8:8 + tm, :] = proj(_OFF_XBC, MM_BLOCK)
    yield
    for j in range(n_conv):
        if j + 1 < n_conv:
            xpad_refs[j + 1][8:8 + tm, :] = proj(_OFF_XBC + (j + 1) * MM_BLOCK, MM_BLOCK)
        if fillers:
            fillers.pop(0)()
        conv_block(j)
        yield
    for fill in fillers:
        fill()
        yield
    k_ref[...] = proj(_OFF_K, KV_WIDTH).astype(BF16)
    v_ref[...] = proj(_OFF_V, KV_WIDTH).astype(BF16)
    dt_ref[...] = proj(_OFF_DT, DT_PAD)


def _ssd_chunk(rows, xc_ref, z_ref, dt_ref, dtb_ref, alog_ref, dskip_ref, normw_ref, y_ref, state_ref, ybuf_ref):
    T = CHUNK
    dt_in = dt_ref[rows, :] + dtb_ref[...]
    dt = jnp.maximum(dt_in, 0.0) + jnp.log(1.0 + jnp.exp(-jnp.abs(dt_in)))
    a = dt * (-jnp.exp(alog_ref[...]))
    row = lax.broadcasted_iota(jnp.int32, (T, T), 0)
    col = lax.broadcasted_iota(jnp.int32, (T, T), 1)
    causal = row >= col
    tril = jnp.where(causal, 1.0, 0.0).astype(F32)
    acum = jnp.dot(tril, a, preferred_element_type=F32, precision=lax.Precision.HIGHEST)
    total = acum[T - 1:T, :]
    acol2 = acum * LOG2E
    arow2 = ((acum - jnp.log(dt)) * LOG2E).T
    wst_t = (jnp.exp(total - acum) * dt).T
    e_acum = jnp.exp(acum)
    e_total = jnp.exp(total)
    lane_lo = lax.broadcasted_iota(jnp.int32, (T, LANES), 1) < HEAD_DIM
    yield

    for g in range(SSD_GROUPS):
        b_off = SSD_WIDTH + g * SSD_STATE
        c_off = SSD_WIDTH + SSD_GROUPS * SSD_STATE + g * SSD_STATE
        b_g = xc_ref[rows, b_off:b_off + SSD_STATE]
        c_g = xc_ref[rows, c_off:c_off + SSD_STATE]
        cb = lax.dot_general(c_g.astype(BF16), b_g.astype(BF16), (((1,), (1,)), ((), ())),
                             preferred_element_type=F32)
        b_t = b_g.T
        s_prev = state_ref[g]
        y_off = jnp.dot(c_g.astype(BF16), s_prev.astype(BF16), preferred_element_type=F32)
        for j in range(4):
            h0 = g * 8 + 2 * j
            slab = g * 4 + j
            xs_pair = xc_ref[rows, slab * LANES:(slab + 1) * LANES]
            rhs = jnp.concatenate([jnp.where(lane_lo, xs_pair, 0.0),
                                   jnp.where(lane_lo, 0.0, xs_pair)], axis=0).astype(BF16)
            m_parts, s_parts = [], []
            for hh in (h0, h0 + 1):
                diff = acol2[:, hh:hh + 1] - arow2[hh:hh + 1, :]
                m_parts.append(cb * jnp.exp2(jnp.where(causal, diff, -jnp.inf)))
                s_parts.append(b_t * wst_t[hh:hh + 1, :])
            lhs_y = jnp.concatenate(m_parts, axis=1).astype(BF16)
            lhs_s = jnp.concatenate(s_parts, axis=1).astype(BF16)
            y_diag = jnp.dot(lhs_y, rhs, preferred_element_type=F32)
            st_new = jnp.dot(lhs_s, rhs, preferred_element_type=F32)
            off_scale = jnp.where(lane_lo, e_acum[:, h0:h0 + 1], e_acum[:, h0 + 1:h0 + 2])
            y_pair = (y_diag + off_scale * y_off[:, j * LANES:(j + 1) * LANES]
                      + dskip_ref[:, slab * LANES:(slab + 1) * LANES] * xs_pair)
            ybuf_ref[rows, slab * LANES:(slab + 1) * LANES] = y_pair
            cd = jnp.where(lane_lo[0:1, :], e_total[:, h0:h0 + 1], e_total[:, h0 + 1:h0 + 2])
            state_ref[g, :, j * LANES:(j + 1) * LANES] = s_prev[:, j * LANES:(j + 1) * LANES] * cd + st_new
        yield

    gw = SSD_WIDTH // SSD_GROUPS
    for g in range(SSD_GROUPS):
        hgate = ybuf_ref[rows, g * gw:(g + 1) * gw] * z_ref[rows, g * gw:(g + 1) * gw]
        ms = jnp.mean(hgate * hgate, axis=-1, keepdims=True)
        y_ref[rows, g * gw:(g + 1) * gw] = (hgate * lax.rsqrt(ms + RMS_EPS)
                                            * normw_ref[:, g * gw:(g + 1) * gw]).astype(BF16)


def _attn_chunk(rows, k_prev, v_prev, first, sinks_ref, q_ref, k_ref, v_ref, y_ref, bias_ref):
    T = CHUNK
    kk = jnp.concatenate([k_prev, k_ref[rows, :]], axis=0).astype(F32)
    vv = jnp.concatenate([v_prev, v_ref[rows, :]], axis=0).astype(F32)
    kk_sw = pltpu.roll(kk, HEAD_DIM, axis=1)
    vv_sw = pltpu.roll(vv, HEAD_DIM, axis=1)
    lane_lo = lax.broadcasted_iota(jnp.int32, (T, LANES), 1) < HEAD_DIM
    lane2_lo = lax.broadcasted_iota(jnp.int32, (2 * T, LANES), 1) < HEAD_DIM
    row4 = lax.broadcasted_iota(jnp.int32, (4 * T, LANES), 0)
    lane4 = lax.broadcasted_iota(jnp.int32, (4 * T, LANES), 1)
    ones_cols = jnp.where((row4 < 2 * T) == (lane4 < HEAD_DIM), 1.0, 0.0).astype(BF16)

    for g in range(ATTN_KV_HEADS):
        own, other = (kk, kk_sw) if g == 0 else (kk_sw, kk)
        kk2 = jnp.concatenate([jnp.where(lane2_lo, own, 0.0), jnp.where(lane2_lo, 0.0, other)],
                              axis=0).astype(BF16)
        own, other = (vv, vv_sw) if g == 0 else (vv_sw, vv)
        vv2 = jnp.concatenate([jnp.where(lane2_lo, own, 0.0), jnp.where(lane2_lo, 0.0, other)],
                              axis=0).astype(BF16)
        vv2 = jnp.concatenate([vv2, ones_cols], axis=1)
        qs = jnp.concatenate([q_ref[rows, (g * 4 + j) * LANES:(g * 4 + j + 1) * LANES] for j in range(4)], axis=0)
        s_all = lax.dot_general(qs, kk2, (((1,), (1,)), ((), ())), preferred_element_type=F32)
        p_rows, sink_rows = [], []
        for j in range(4):
            p_halves, sink_halves = [], []
            for half in range(2):
                hh = g * 8 + 2 * j + half
                s = s_all[j * T:(j + 1) * T, half * 2 * T:(half + 1) * 2 * T] + bias_ref[first, hh]
                sink2 = sinks_ref[hh] * LOG2E
                m = jnp.maximum(jnp.max(s, axis=-1, keepdims=True), sink2)
                p_halves.append(jnp.exp2(s - m).astype(BF16))
                sink_halves.append(jnp.exp2(sink2 - m))
            p_rows.append(jnp.concatenate(p_halves, axis=1))
            sink_rows.append(jnp.where(lane_lo, sink_halves[0], sink_halves[1]))
        p_all = jnp.concatenate(p_rows, axis=0)
        o_all = jnp.dot(p_all, vv2, preferred_element_type=F32)
        for j in range(4):
            col0 = SSD_WIDTH + (g * 4 + j) * LANES
            o_j = o_all[j * T:(j + 1) * T, :]
            den = o_j[:, LANES:] + sink_rows[j]
            y_ref[rows, col0:col0 + LANES] = (o_j[:, :LANES] * (1.0 / den)).astype(BF16)
        yield


def _init_attn_bias(bias_ref):
    T = CHUNK
    qi = lax.broadcasted_iota(jnp.int32, (T, 2 * T), 0)
    kj = lax.broadcasted_iota(jnp.int32, (T, 2 * T), 1)
    dist = qi + T - kj
    band = (dist >= 0) & (dist < T)
    dist_f = dist.astype(F32)
    for hh in range(ATTN_HEADS):
        alibi = dist_f * (-SLOPES[hh] * LOG2E)
        bias_ref[0, hh] = jnp.where(band, alibi, -jnp.inf)
        bias_ref[1, hh] = jnp.where(band & (kj >= T), alibi, -jnp.inf)


def _mixer_body(seq_start, sinks_ref, z_ref, xbc_ref, dt_ref, q_ref, k_ref, v_ref,
                dtb_ref, alog_ref, dskip_ref, normw_ref, y_ref,
                state_ref, kprev_ref, vprev_ref, ybuf_ref, bias_ref):
    T = CHUNK
    for ci in range(y_ref.shape[0] // T):
        rows = pl.ds(ci * T, T)
        yield from _ssd_chunk(rows, xbc_ref, z_ref, dt_ref, dtb_ref, alog_ref, dskip_ref, normw_ref, y_ref,
                              state_ref, ybuf_ref)
        yield
        if ci == 0:
            k_prev, v_prev = kprev_ref[...], vprev_ref[...]
            first = jnp.where(seq_start, 1, 0)
        else:
            prev_rows = pl.ds((ci - 1) * T, T)
            k_prev, v_prev = k_ref[prev_rows, :], v_ref[prev_rows, :]
            first = 0
        yield from _attn_chunk(rows, k_prev, v_prev, first, sinks_ref, q_ref, k_ref, v_ref, y_ref, bias_ref)


_DONE = object()
_PROJ_PIECES = [(SSD_WIDTH, F32), (CONV_DIM, F32), (ATTN_WIDTH, BF16), (KV_WIDTH, BF16), (KV_WIDTH, BF16),
                (DT_PAD, F32)]


W_IN_STAGE = 256


def _load_w_in_bf16(wt_hbm, w_ref, stage_ref, sem_ref):
    i2 = SSD_WIDTH + CONV_DIM
    i3 = i2 + SSD_HEADS
    blocks = [(c, c) for c in range(0, i2, W_IN_STAGE)]
    blocks += [(i3 + c, i2 + c) for c in range(0, _OFF_DT - i2, W_IN_STAGE)]

    def copy(i):
        src = wt_hbm.at[0, pl.ds(blocks[i][0], W_IN_STAGE), :]
        return pltpu.make_async_copy(src, stage_ref.at[i % 2], sem_ref.at[i % 2])

    stage_ref[2, 0:DT_PAD, :] = jnp.zeros((DT_PAD, D_MODEL), F32)
    dt_copy = pltpu.make_async_copy(wt_hbm.at[0, pl.ds(i2, SSD_HEADS), :], stage_ref.at[2, pl.ds(0, SSD_HEADS), :],
                                    sem_ref.at[2])
    dt_copy.start()
    copy(0).start()
    for i in range(len(blocks)):
        if i + 1 < len(blocks):
            copy(i + 1).start()
        copy(i).wait()
        dst = blocks[i][1]
        w_ref[:, dst:dst + W_IN_STAGE] = stage_ref[i % 2].T.astype(BF16)
    dt_copy.wait()
    w_ref[:, _OFF_DT:_OFF_DT + DT_PAD] = stage_ref[2, 0:DT_PAD, :].T.astype(BF16)


def _proj_mixer_kernel(sinks_ref, x_ref, mod_ref, g_ref, b_ref, wt_hbm, convw_ref, convb_ref,
                       dtb_ref, alog_ref, dskip_ref, normw_ref, wout_ref, w1_ref, w2_ref,
                       y_ref, wout_bf_ref, w1_bf_ref, w2_bf_ref,
                       w_ref, wstage_ref, wsem_ref, *scratch, tiles_per_seq, n_tiles):
    staged = scratch[:len(_PROJ_PIECES)]
    xpad_refs = scratch[len(_PROJ_PIECES):len(_PROJ_PIECES) + N_CONV_BLOCKS]
    state_ref, kprev_ref, vprev_ref, ybuf_ref, bias_ref = scratch[len(_PROJ_PIECES) + N_CONV_BLOCKS:]
    s = pl.program_id(0)
    wr = s % 2
    rd = 1 - wr
    mix_seq_start = (s - 1) % tiles_per_seq == 0

    @pl.when(s == 0)
    def _():
        _load_w_in_bf16(wt_hbm, w_ref, wstage_ref, wsem_ref)
        _init_attn_bias(bias_ref)

    for src, dst in ((wout_ref, wout_bf_ref), (w1_ref, w1_bf_ref), (w2_ref, w2_bf_ref)):
        dst[...] = src[...].astype(BF16)

    @pl.when(s % tiles_per_seq == 0)
    def _():
        for xpad_ref in xpad_refs:
            xpad_ref[0:8, :] = jnp.zeros((8, MM_BLOCK), F32)

    @pl.when(mix_seq_start)
    def _():
        state_ref[...] = jnp.zeros_like(state_ref)
        kprev_ref[...] = jnp.zeros_like(kprev_ref)
        vprev_ref[...] = jnp.zeros_like(vprev_ref)

    def run(project, mix):
        works = []
        if project:
            works.append(_inproj_body(x_ref, mod_ref, g_ref, b_ref, w_ref, convw_ref, convb_ref,
                                      *[ref.at[wr] for ref in staged], xpad_refs))
        if mix:
            z_s, xbc_s, q_s, k_s, v_s, dt_s = [ref.at[rd] for ref in staged]
            works.append(_mixer_body(mix_seq_start, sinks_ref, z_s, xbc_s, dt_s, q_s, k_s, v_s,
                                     dtb_ref, alog_ref, dskip_ref, normw_ref, y_ref,
                                     state_ref, kprev_ref, vprev_ref, ybuf_ref, bias_ref))
        while works:
            for work in list(works):
                if next(work, _DONE) is _DONE:
                    works.remove(work)
        if mix:
            last_rows = pl.ds(y_ref.shape[0] - CHUNK, CHUNK)
            kprev_ref[...] = k_s[last_rows, :]
            vprev_ref[...] = v_s[last_rows, :]

    pl.when(s == 0)(lambda: run(project=True, mix=False))
    pl.when(s > 0)(lambda: run(project=True, mix=True))


def _proj_mixer(sinks, x2, mod3, ln_g, ln_b, w_in_t, conv_w, conv_b, dtb, alog, dskip, normw, next_weights, seq, tm):
    n_tok = x2.shape[0]
    n_tiles = n_tok // tm
    tiles_per_seq = seq // tm
    const = lambda s: (0, 0)
    proj_tile = lambda s: jnp.minimum(s, n_tiles - 1)
    pspec = lambda r, w: pl.BlockSpec((r, w), const)
    wspec = lambda w: pl.BlockSpec((w.shape[0] // n_tiles, w.shape[1]), lambda s: (proj_tile(s), 0))
    return pl.pallas_call(
        functools.partial(_proj_mixer_kernel, tiles_per_seq=tiles_per_seq, n_tiles=n_tiles),
        grid=(n_tiles + 1,),
        in_specs=[pl.BlockSpec(memory_space=pltpu.SMEM),
                  pl.BlockSpec((tm, D_MODEL), lambda s: (proj_tile(s), 0)),
                  pl.BlockSpec((None, 1, 6 * D_MODEL), lambda s: (proj_tile(s) // tiles_per_seq, 0, 0)),
                  pspec(1, D_MODEL), pspec(1, D_MODEL),
                  pl.BlockSpec(memory_space=pl.ANY),
                  pspec(CONV_K, CONV_DIM), pspec(1, CONV_DIM),
                  pspec(1, DT_PAD), pspec(1, DT_PAD), pspec(1, SSD_WIDTH), pspec(1, SSD_WIDTH)]
                 + [wspec(w) for w in next_weights],
        out_specs=[pl.BlockSpec((tm, MIX_WIDTH), lambda s: (jnp.maximum(s - 1, 0), 0))]
                  + [wspec(w) for w in next_weights],
        out_shape=[jax.ShapeDtypeStruct((n_tok, MIX_WIDTH), BF16)]
                  + [jax.ShapeDtypeStruct(w.shape, BF16) for w in next_weights],
        scratch_shapes=([pltpu.VMEM((D_MODEL, PROJ_PAD), BF16),
                         pltpu.VMEM((3, W_IN_STAGE, D_MODEL), F32),
                         pltpu.SemaphoreType.DMA((3,))]
                        + [pltpu.VMEM((2, tm, w), dt) for w, dt in _PROJ_PIECES]
                        + [pltpu.VMEM((tm + 8, MM_BLOCK), F32)] * N_CONV_BLOCKS
                        + [pltpu.VMEM((SSD_GROUPS, SSD_STATE, SSD_WIDTH // SSD_GROUPS), F32),
                           pltpu.VMEM((CHUNK, KV_WIDTH), BF16),
                           pltpu.VMEM((CHUNK, KV_WIDTH), BF16),
                           pltpu.VMEM((tm, SSD_WIDTH), F32),
                           pltpu.VMEM((2, ATTN_HEADS, CHUNK, 2 * CHUNK), F32)]),
        compiler_params=pltpu.CompilerParams(dimension_semantics=("arbitrary",),
                                             vmem_limit_bytes=VMEM_LIMIT, flags=INPROJ_FLAGS),
        name="proj_mixer",
    )(sinks, x2, mod3, ln_g, ln_b, w_in_t, conv_w, conv_b, dtb, alog, dskip, normw, *next_weights)


FF_CHUNK = 1024


def _out_ffn_rows(rows, x_ref, y_ref, mod_ref, lng_ref, lnb_ref, wout_ref, ln1g_ref, ln1b_ref,
                  w1_ref, b1_ref, w2_ref, b2_ref, ln2g_ref, ln2b_ref, o_ref):
    h0 = _layer_norm(x_ref[rows, :], lng_ref[...], lnb_ref[...])
    mix = jnp.dot(y_ref[rows, :], wout_ref[...], preferred_element_type=F32)
    yield
    g1 = mod_ref[:, 2 * D_MODEL:3 * D_MODEL]
    sh2 = mod_ref[:, 3 * D_MODEL:4 * D_MODEL]
    sc2 = mod_ref[:, 4 * D_MODEL:5 * D_MODEL]
    g2 = mod_ref[:, 5 * D_MODEL:6 * D_MODEL]
    h1 = _layer_norm(ALPHA * h0 + (1.0 + g1) * mix, ln1g_ref[...], ln1b_ref[...])
    u = (h1 * (1.0 + sc2) + sh2).astype(BF16)
    yield
    f = jnp.zeros_like(h1)
    for ck in range(D_FF // FF_CHUNK):
        sl = slice(ck * FF_CHUNK, (ck + 1) * FF_CHUNK)
        hid = jnp.dot(u, w1_ref[:, sl], preferred_element_type=F32) + b1_ref[:, sl]
        hid = jnp.square(jnp.maximum(hid, 0.0)).astype(BF16)
        f = f + jnp.dot(hid, w2_ref[sl, :], preferred_element_type=F32)
        yield
    f = f + b2_ref[...]
    o_ref[rows, :] = _layer_norm(ALPHA * h1 + (1.0 + g2) * f, ln2g_ref[...], ln2b_ref[...])


FFN_ROW_GROUPS = 2


def _out_ffn_kernel(x_ref, *refs):
    rows = x_ref.shape[0] // FFN_ROW_GROUPS
    waiting = [_out_ffn_rows(pl.ds(g * rows, rows), x_ref, *refs) for g in range(FFN_ROW_GROUPS)]
    running = []
    while waiting or running:
        if waiting:
            running.insert(0, waiting.pop(0))
        for work in list(running):
            if next(work, _DONE) is _DONE:
                running.remove(work)


def _out_ffn(x2, ycat, mod3, ln_g, ln_b, w_out, ln1_g, ln1_b, w1, b1, w2, b2, ln2_g, ln2_b, seq, tm):
    n_tok = x2.shape[0]
    tiles_per_seq = seq // tm
    row = lambda i: (i, 0)
    const = lambda i: (0, 0)
    resident = lambda r, w: pl.BlockSpec((r, w), const, pipeline_mode=pl.Buffered(1))
    return pl.pallas_call(
        _out_ffn_kernel,
        grid=(n_tok // tm,),
        in_specs=[pl.BlockSpec((tm, D_MODEL), row),
                  pl.BlockSpec((tm, MIX_WIDTH), row),
                  pl.BlockSpec((None, 1, 6 * D_MODEL), lambda i: (i // tiles_per_seq, 0, 0)),
                  resident(1, D_MODEL), resident(1, D_MODEL),
                  resident(MIX_WIDTH, D_MODEL),
                  resident(1, D_MODEL), resident(1, D_MODEL),
                  resident(D_MODEL, D_FF), resident(1, D_FF),
                  resident(D_FF, D_MODEL), resident(1, D_MODEL),
                  resident(1, D_MODEL), resident(1, D_MODEL)],
        out_specs=pl.BlockSpec((tm, D_MODEL), row),
        out_shape=jax.ShapeDtypeStruct((n_tok, D_MODEL), F32),
        compiler_params=pltpu.CompilerParams(dimension_semantics=("arbitrary",),
                                             vmem_limit_bytes=VMEM_LIMIT),
        name="out_ffn",
    )(x2, ycat, mod3, ln_g, ln_b, w_out, ln1_g, ln1_b, w1, b1, w2, b2, ln2_g, ln2_b)


def kernel(x, c, ln_in_g, ln_in_b, ada_w, ada_b, w_in, conv_w, conv_b, dt_bias, a_log, d_skip, ssd_norm_w,
           attn_sinks, w_out, ln1_g, ln1_b, w_ff1, b_ff1, w_ff2, b_ff2, ln2_g, ln2_b):
    batch, seq, _ = x.shape
    depth = ada_w.shape[0]
    assert depth == 1 and seq % CHUNK == 0
    n_tok = batch * seq
    row2 = lambda p: p.reshape(1, -1)
    x2 = x.reshape(n_tok, D_MODEL)
    ln_g, ln_b = row2(ln_in_g), row2(ln_in_b)

    c_pad = jnp.pad(c, ((0, 8 - batch), (0, 0)))
    mod = _adaln_mod(c_pad, ada_w[0], row2(ada_b[0]))
    mod3 = mod[:batch].reshape(batch, 1, 6 * D_MODEL)

    pad_heads = lambda p: jnp.pad(p, (0, DT_PAD - SSD_HEADS)).reshape(1, DT_PAD)
    ycat, w_out_bf, w_ff1_bf, w_ff2_bf = _proj_mixer(attn_sinks[0], x2, mod3, ln_g, ln_b, jnp.swapaxes(w_in, 1, 2), conv_w[0], row2(conv_b[0]),
                       pad_heads(dt_bias[0]), pad_heads(a_log[0]),
                       jnp.repeat(d_skip[0], HEAD_DIM).reshape(1, SSD_WIDTH), row2(ssd_norm_w[0]),
                       (w_out[0], w_ff1[0], w_ff2[0]), seq, tm=512)

    out = _out_ffn(x2, ycat, mod3, ln_g, ln_b, w_out_bf, row2(ln1_g[0]), row2(ln1_b[0]),
                   w_ff1_bf, row2(b_ff1[0]), w_ff2_bf, row2(b_ff2[0]),
                   row2(ln2_g[0]), row2(ln2_b[0]), seq, tm=512)
    return out.reshape(batch, seq, D_MODEL)
```
